```python
import jax, jax.numpy as jnp
from jax import lax
import numpy as np

D_MODEL = 1024
BATCH = 8
SEQ = 2048
DEPTH = 4

MEM_LEN = 256
HEAD_DIM = 64
POOL_WINDOWS = (2, 4, 8, 16)
POOL_WIDTH = D_MODEL // 4
POOL_GROUP = POOL_WIDTH // len(POOL_WINDOWS)
DSA_HEADS = 8
DSA_WIDTH = DSA_HEADS * HEAD_DIM
DSA_PATTERNS = ((128, 1), (512, 4), (2048, 16))
DSA_BLOCK = 128
MEM_HEADS = 4
MEM_WIDTH = MEM_HEADS * HEAD_DIM
MIX_WIDTH = POOL_WIDTH + DSA_WIDTH + MEM_WIDTH
IN_WIDTH = POOL_WIDTH + 3 * DSA_WIDTH + MEM_WIDTH
SPLITS = (POOL_WIDTH, POOL_WIDTH + DSA_WIDTH, POOL_WIDTH + 2 * DSA_WIDTH, POOL_WIDTH + 3 * DSA_WIDTH)
ROPE_THETA = 500000.0
ROT_DIM = HEAD_DIM // 4
N_EXPERTS = 64
TOP_K = 8
N_GROUPS = 8
TOPK_GROUPS = 4
D_EXPERT = 256
D_SHARED = 256
ROUTED_SCALE = 2.5
MOE_BLOCK = 128
LN_EPS = 1e-5
DEEPNORM_ALPHA = (2 * DEPTH) ** 0.25
DEEPNORM_BETA = (8 * DEPTH) ** -0.25

kernel_name = "hybrid_pool_dilated_mem_moe_deepnorm"

F32 = jnp.float32


def layer_norm(x, g, b):
    xf = x.astype(F32)
    mu = xf.mean(-1, keepdims=True)
    var = jnp.square(xf - mu).mean(-1, keepdims=True)
    return ((xf - mu) * lax.rsqrt(var + LN_EPS) * g.astype(F32) + b.astype(F32)).astype(x.dtype)


def rope_tables(S):
    pos = jnp.arange(S, dtype=F32)
    inv = ROPE_THETA ** (-jnp.arange(0, ROT_DIM, 2, dtype=F32) / ROT_DIM)
    ang = pos[:, None] * inv[None, :]
    return jnp.cos(ang), jnp.sin(ang)


def partial_rotary(x, cos, sin):
    half = ROT_DIM // 2
    xr = x[..., :ROT_DIM].astype(F32)
    x1, x2 = xr[..., :half], xr[..., half:]
    c, s = cos[None, :, None, :], sin[None, :, None, :]
    rot = jnp.concatenate([x1 * c - x2 * s, x2 * c + x1 * s], axis=-1).astype(x.dtype)
    return jnp.concatenate([rot, x[..., ROT_DIM:]], axis=-1)


def multiscale_pool(u, w_pool, pool_scale):
    B, S, _ = u.shape
    grp = u.astype(F32).reshape(B, S, len(POOL_WINDOWS), POOL_GROUP)
    csum = jnp.cumsum(grp, axis=1)
    t = jnp.arange(S)
    outs = []
    for g, w in enumerate(POOL_WINDOWS):
        cg = csum[:, :, g]
        lag = jnp.pad(cg, ((0, 0), (w, 0), (0, 0)))[:, :S]
        cnt = jnp.minimum(t + 1, w).astype(F32)[None, :, None]
        outs.append((cg - lag) / cnt - grp[:, :, g])
    pooled = jnp.stack(outs, axis=2).astype(u.dtype)
    mixed = jnp.einsum('bsgc,gcd->bsgd', pooled, w_pool)
    return mixed.reshape(B, S, POOL_WIDTH) * pool_scale


def dilated_window_branch(q, k, v, dilation, lookback):
    B, S, H, Dh = q.shape
    L = S // dilation
    nb = -(-L // DSA_BLOCK)
    Lp = nb * DSA_BLOCK

    def split(a):
        a = a.reshape(B, L, dilation, H, Dh)
        return jnp.pad(a, ((0, 0), (0, Lp - L), (0, 0), (0, 0), (0, 0)))

    def band(a):
        a = jnp.pad(split(a), ((0, 0), (DSA_BLOCK, 0), (0, 0), (0, 0), (0, 0)))
        a = a.reshape(B, nb + 1, DSA_BLOCK, dilation, H, Dh)
        return jnp.concatenate([a[:, :-1], a[:, 1:]], axis=2)

    qs = split(q).reshape(B, nb, DSA_BLOCK, dilation, H, Dh).astype(F32)
    kb, vb = band(k).astype(F32), band(v).astype(F32)
    s = jnp.einsum('bnqrhd,bnkrhd->bnrhqk', qs, kb) * (Dh ** -0.5)
    qi = jnp.arange(DSA_BLOCK)[:, None]
    kj = jnp.arange(2 * DSA_BLOCK)[None, :]
    diff = qi + DSA_BLOCK - kj
    key_row = (jnp.arange(nb)[:, None, None] - 1) * DSA_BLOCK + kj[None]
    valid = ((diff >= 0) & (diff <= lookback))[None] & (key_row >= 0)
    s = jnp.where(valid[None, :, None, None], s, -jnp.inf)
    m = s.max(-1, keepdims=True)
    p = jnp.exp(s - m)
    den = p.sum(-1, keepdims=True)
    o = jnp.einsum('bnrhqk,bnkrhd->bnqrhd', p / den, vb)
    lse = (m + jnp.log(den))[..., 0]
    o = o.reshape(B, Lp, dilation, H, Dh)[:, :L].reshape(B, S, H, Dh)
    lse = lse.transpose(0, 1, 4, 2, 3).reshape(B, Lp, dilation, H)[:, :L].reshape(B, S, H)
    return o, lse


def dilated_attention(q, k, v):
    outs, lses = [], []
    for window, dilation in DSA_PATTERNS:
        o, lse = dilated_window_branch(q, k, v, dilation, window // dilation)
        outs.append(o)
        lses.append(lse)
    wts = jax.nn.softmax(jnp.stack(lses, axis=0), axis=0)
    return jnp.einsum('pbsh,pbshd->bshd', wts, jnp.stack(outs, axis=0))


def memory_attention(q, k, v):
    s = jnp.einsum('bshd,bmhd->bhsm', q.astype(F32), k.astype(F32)) * (HEAD_DIM ** -0.5)
    p = jax.nn.softmax(s, axis=-1)
    return jnp.einsum('bhsm,bmhd->bshd', p, v.astype(F32))


def hybrid_mixer(x, mem, w_in, w_pool, pool_scale, w_mem_kv, w_out, cos, sin):
    B, S, _ = x.shape
    h = x @ w_in
    u, q, k, v, qm = jnp.split(h, list(SPLITS), axis=-1)
    heads = lambda a: a.reshape(a.shape[0], a.shape[1], -1, HEAD_DIM)
    q = partial_rotary(heads(q), cos, sin)
    k = partial_rotary(heads(k), cos, sin)
    y_pool = multiscale_pool(u, w_pool, pool_scale)
    y_dsa = dilated_attention(q, k, heads(v)).astype(x.dtype).reshape(B, S, DSA_WIDTH)
    km, vm = jnp.split(mem @ w_mem_kv, 2, axis=-1)
    y_mem = memory_attention(heads(qm), heads(km), heads(vm)).astype(x.dtype).reshape(B, S, MEM_WIDTH)
    return jnp.concatenate([y_pool, y_dsa, y_mem], axis=-1) @ w_out


def swiglu(x, wg, wu, wd):
    return (jax.nn.silu(x @ wg) * (x @ wu)) @ wd


def route(xf, w_router, router_bias):
    N = xf.shape[0]
    scores = jax.nn.sigmoid((xf @ w_router).astype(F32))
    biased = scores + router_bias.astype(F32)
    grp_score = lax.top_k(biased.reshape(N, N_GROUPS, -1), 2)[0].sum(-1)
    _, grp_idx = lax.top_k(grp_score, TOPK_GROUPS)
    grp_mask = jax.nn.one_hot(grp_idx, N_GROUPS, dtype=F32).sum(-2) > 0
    expert_mask = jnp.repeat(grp_mask, N_EXPERTS // N_GROUPS, axis=-1)
    _, top_idx = lax.top_k(jnp.where(expert_mask, biased, -jnp.inf), TOP_K)
    top_w = jnp.take_along_axis(scores, top_idx, axis=-1)
    top_w = top_w / top_w.sum(-1, keepdims=True) * ROUTED_SCALE
    return top_idx, top_w


def routed_experts(xf, top_idx, top_w, w_gate, w_up, w_down):
    N, D = xf.shape
    A = N * TOP_K
    e_flat = top_idx.reshape(A)
    tok_flat = jnp.repeat(jnp.arange(N, dtype=jnp.int32), TOP_K)
    order = jnp.argsort(e_flat)
    e_sorted, tok_sorted = e_flat[order], tok_flat[order]
    w_sorted = top_w.reshape(A)[order]
    counts = jnp.bincount(e_flat, length=N_EXPERTS)
    padded = (counts + MOE_BLOCK - 1) // MOE_BLOCK * MOE_BLOCK
    pend = jnp.cumsum(padded)
    pstart = pend - padded
    start = jnp.cumsum(counts) - counts
    dest = pstart[e_sorted] + jnp.arange(A) - start[e_sorted]
    n_blocks = -(-A // MOE_BLOCK) + N_EXPERTS
    rows_tok = jnp.zeros((n_blocks * MOE_BLOCK,), jnp.int32).at[dest].set(tok_sorted)
    block_e = jnp.minimum(jnp.searchsorted(pend, jnp.arange(n_blocks) * MOE_BLOCK, side='right'), N_EXPERTS - 1)
    xb = xf[rows_tok].reshape(n_blocks, MOE_BLOCK, D)

    def expert_block(args):
        xblk, e = args
        return swiglu(xblk, w_gate[e], w_up[e], w_down[e])

    yb = lax.map(expert_block, (xb, block_e)).reshape(n_blocks * MOE_BLOCK, D)
    contrib = yb[dest] * w_sorted[:, None].astype(yb.dtype)
    return jax.ops.segment_sum(contrib, tok_sorted, num_segments=N)


def moe_ffn(x, w_router, router_bias, w_gate, w_up, w_down, ws_gate, ws_up, ws_down):
    B, S, D = x.shape
    xf = x.reshape(B * S, D)
    top_idx, top_w = route(xf, w_router, router_bias)
    y = routed_experts(xf, top_idx, top_w, w_gate, w_up, w_down) + swiglu(xf, ws_gate, ws_up, ws_down)
    return y.reshape(B, S, D)


def setup_inputs(seed: int = 0) -> dict:
    key = jax.random.key(seed)
    ks = jax.random.split(key, 20)
    nrm = lambda k, shape, scale: jax.random.normal(k, shape, F32) * scale
    L, D, E = DEPTH, D_MODEL, N_EXPERTS
    return {
        "x": nrm(ks[0], (BATCH, SEQ, D), 1.0),
        "mem": nrm(ks[1], (BATCH, MEM_LEN, D), 1.0),
        "w_in": nrm(ks[2], (L, D, IN_WIDTH), D ** -0.5),
        "w_pool": nrm(ks[3], (L, len(POOL_WINDOWS), POOL_GROUP, POOL_GROUP), POOL_GROUP ** -0.5),
        "pool_scale": 1.0 + nrm(ks[4], (L, POOL_WIDTH), 0.1),
        "w_mem_kv": nrm(ks[5], (L, D, 2 * MEM_WIDTH), D ** -0.5),
        "w_out": nrm(ks[6], (L, MIX_WIDTH, D), MIX_WIDTH ** -0.5 * DEEPNORM_BETA),
        "ln1_g": 1.0 + nrm(ks[7], (L, D), 0.02),
        "ln1_b": nrm(ks[8], (L, D), 0.02),
        "w_router": nrm(ks[9], (L, D, E), D ** -0.5),
        "router_bias": nrm(ks[10], (L, E), 0.01),
        "w_gate": nrm(ks[11], (L, E, D, D_EXPERT), D ** -0.5),
        "w_up": nrm(ks[12], (L, E, D, D_EXPERT), D ** -0.5),
        "w_down": nrm(ks[13], (L, E, D_EXPERT, D), D_EXPERT ** -0.5 * DEEPNORM_BETA),
        "ws_gate": nrm(ks[14], (L, D, D_SHARED), D ** -0.5),
        "ws_up": nrm(ks[15], (L, D, D_SHARED), D ** -0.5),
        "ws_down": nrm(ks[16], (L, D_SHARED, D), D_SHARED ** -0.5 * DEEPNORM_BETA),
        "ln2_g": 1.0 + nrm(ks[17], (L, D), 0.02),
        "ln2_b": nrm(ks[18], (L, D), 0.02),
    }


def reference(x, mem, w_in, w_pool, pool_scale, w_mem_kv, w_out, ln1_g, ln1_b, w_router, router_bias,
              w_gate, w_up, w_down, ws_gate, ws_up, ws_down, ln2_g, ln2_b):
    cos, sin = rope_tables(x.shape[1])
    for l in range(DEPTH):
        mix = hybrid_mixer(x, mem, w_in[l], w_pool[l], pool_scale[l], w_mem_kv[l], w_out[l], cos, sin)
        x = layer_norm(DEEPNORM_ALPHA * x + mix, ln1_g[l], ln1_b[l])
        ffn = moe_ffn(x, w_router[l], router_bias[l], w_gate[l], w_up[l], w_down[l],
                      ws_gate[l], ws_up[l], ws_down[l])
        x = layer_norm(DEEPNORM_ALPHA * x + ffn, ln2_g[l], ln2_b[l])
    return x
```

```python
import functools

import jax
import jax.numpy as jnp
from jax import lax
from jax.experimental import pallas as pl
from jax.experimental.pallas import tpu as pltpu

F32 = jnp.float32
BF16 = jnp.bfloat16

D_MODEL = 1024
HEAD_DIM = 64
POOL_WINDOWS = (2, 4, 8, 16)
POOL_WIDTH = 256
POOL_GROUP = 64
DSA_WIDTH = 512
DSA_LOOKBACK = 128
DSA_DILATIONS = (1, 4, 16)
MEM_WIDTH = 256
IN_WIDTH = 2048
ROPE_THETA = 500000.0
ROT_DIM = 16
N_EXPERTS = 64
TOP_K = 8
N_GROUPS = 8
TOPK_GROUPS = 4
D_EXPERT = 256
ROUTED_SCALE = 2.5
LN_EPS = 1e-5

LANES = 128
SUBLANES = 8
ROW_VREGS = D_MODEL // LANES

NEG_BIG = -1e30

TM_PROJ = 512
QBLK = 128
MOE_CHUNK = 4096
MOE_BM = 256
VMEM_LIMIT = 56 * 1024 * 1024


def _cparams(sem):
    return pltpu.CompilerParams(dimension_semantics=sem, vmem_limit_bytes=VMEM_LIMIT)


def _inproj_kernel(x_ref, w_ref, c_ref, s1_ref, s2_ref, u_ref, q_ref, k_ref, v_ref, qm_ref):
    xb = x_ref[...].astype(BF16)
    h = jnp.dot(xb, w_ref[...], preferred_element_type=F32)
    c = c_ref[...]
    s1 = s1_ref[...]
    s2 = s2_ref[...]

    def rot(a):
        return a * c + pltpu.roll(a, LANES - ROT_DIM // 2, 1) * s1 + pltpu.roll(a, ROT_DIM // 2, 1) * s2

    scale = HEAD_DIM ** -0.5
    u_ref[...] = h[:, :POOL_WIDTH]
    q0 = POOL_WIDTH
    k0 = q0 + DSA_WIDTH
    v0 = k0 + DSA_WIDTH
    m0 = v0 + DSA_WIDTH
    for j in range(DSA_WIDTH // LANES):
        q_ref[:, j * LANES:(j + 1) * LANES] = rot(h[:, q0 + j * LANES:q0 + (j + 1) * LANES]) * scale
        k_ref[:, j * LANES:(j + 1) * LANES] = rot(h[:, k0 + j * LANES:k0 + (j + 1) * LANES])
    v_ref[...] = h[:, v0:m0]
    qm_ref[...] = h[:, m0:] * scale


def _in_projection(x2d, w_in_bf, rope_c, rope_s1, rope_s2, seq):
    n = x2d.shape[0]
    tm = TM_PROJ
    per_seq = seq // tm
    row = lambda i: (i, 0)
    tab = lambda i: (i % per_seq, 0)
    return pl.pallas_call(
        _inproj_kernel,
        grid=(n // tm,),
        in_specs=[
            pl.BlockSpec((tm, D_MODEL), row),
            pl.BlockSpec((D_MODEL, IN_WIDTH), lambda i: (0, 0)),
            pl.BlockSpec((tm, LANES), tab),
            pl.BlockSpec((tm, LANES), tab),
            pl.BlockSpec((tm, LANES), tab),
        ],
        out_specs=[
            pl.BlockSpec((tm, POOL_WIDTH), row),
            pl.BlockSpec((tm, DSA_WIDTH), row),
            pl.BlockSpec((tm, DSA_WIDTH), row),
            pl.BlockSpec((tm, DSA_WIDTH), row),
            pl.BlockSpec((tm, MEM_WIDTH), row),
        ],
        out_shape=[
            jax.ShapeDtypeStruct((n, POOL_WIDTH), F32),
            jax.ShapeDtypeStruct((n, DSA_WIDTH), F32),
            jax.ShapeDtypeStruct((n, DSA_WIDTH), F32),
            jax.ShapeDtypeStruct((n, DSA_WIDTH), F32),
            jax.ShapeDtypeStruct((n, MEM_WIDTH), F32),
        ],
        compiler_params=_cparams(("parallel",)),
        name="in_projection",
    )(x2d, w_in_bf, rope_c, rope_s1, rope_s2)


def _two_head_attention(q2, kk, vv, bias):
    s = lax.dot_general(q2, kk, (((1,), (1,)), ((), ())), preferred_element_type=F32)
    if bias is not None:
        s = s + bias
    m = jnp.max(s, axis=-1, keepdims=True)
    p = jnp.exp(s - m)
    l = jnp.sum(p, axis=-1, keepdims=True)
    o2 = jnp.dot(p.astype(BF16), vv, preferred_element_type=F32) / l
    lse2 = m + jnp.log(l)
    first = lax.broadcasted_iota(jnp.int32, (QBLK, LANES), 1) < HEAD_DIM
    o = jnp.where(first, o2[:QBLK], o2[QBLK:])
    lse = jnp.where(first, lse2[:QBLK], lse2[QBLK:])
    return o, lse


def _band_bias(nk, dbase):
    qi = lax.broadcasted_iota(jnp.int32, (2 * QBLK, nk), 0) & (QBLK - 1)
    kj = lax.broadcasted_iota(jnp.int32, (2 * QBLK, nk), 1)
    diff = qi - kj + dbase
    ok = lax.bitcast_convert_type(diff, jnp.uint32) <= jnp.uint32(DSA_LOOKBACK)
    return jnp.where(ok, 0.0, NEG_BIG).astype(F32)


def _dsa_kernel(q_ref, k_ref, v_ref, o_ref,
                qa1, qb1, k1, v1, qa4, qb4, k4, v4, qa16, qb16, k16, v16,
                o1, l1, o4, l4, o16, l16):
    seq = q_ref.shape[0]

    def stage(dil, qa, qb, kd, vd):
        sub = seq // dil
        for r in range(dil):
            if dil == 1:
                qv, kv, vv = q_ref[...], k_ref[...], v_ref[...]
            else:
                qv = q_ref[pl.ds(r, sub, stride=dil), :]
                kv = k_ref[pl.ds(r, sub, stride=dil), :]
                vv = v_ref[pl.ds(r, sub, stride=dil), :]
            fm = lax.broadcasted_iota(jnp.int32, (sub, LANES), 1) < HEAD_DIM
            qa[r * sub:(r + 1) * sub, :] = jnp.where(fm, qv, 0.0).astype(BF16)
            qb[r * sub:(r + 1) * sub, :] = jnp.where(fm, 0.0, qv).astype(BF16)
            kd[r * sub:(r + 1) * sub, :] = kv.astype(BF16)
            vd[r * sub:(r + 1) * sub, :] = vv.astype(BF16)

    stage(1, qa1, qb1, k1, v1)
    stage(4, qa4, qb4, k4, v4)
    stage(16, qa16, qb16, k16, v16)

    n_blocks = seq // QBLK

    def run(qa, qb, kd, vd, od, ld, blocks_per_seq, nk):
        def body(i, carry):
            qrow = pl.multiple_of(i * QBLK, QBLK)
            has_prev = jnp.where((i % blocks_per_seq) > 0, 1, 0) if nk > QBLK else 0
            krow = pl.multiple_of((i - has_prev) * QBLK, QBLK)
            q2 = jnp.concatenate([qa[pl.ds(qrow, QBLK), :], qb[pl.ds(qrow, QBLK), :]], axis=0)
            kk = kd[pl.ds(krow, nk), :]
            vv = vd[pl.ds(krow, nk), :]
            o, lse = _two_head_attention(q2, kk, vv, _band_bias(nk, has_prev * QBLK))
            od[pl.ds(qrow, QBLK), :] = o
            ld[pl.ds(qrow, QBLK), :] = lse
            return carry
        lax.fori_loop(0, n_blocks, body, 0)

    run(qa1, qb1, k1, v1, o1, l1, n_blocks, 2 * QBLK)
    run(qa4, qb4, k4, v4, o4, l4, n_blocks // 4, 2 * QBLK)
    run(qa16, qb16, k16, v16, o16, l16, 1, QBLK)

    sub16 = seq // 16
    sub4 = seq // 4
    for r in range(16):
        a_o = o1[pl.ds(r, sub16, stride=16), :]
        a_l = l1[pl.ds(r, sub16, stride=16), :]
        b_start = (r % 4) * sub4 + r // 4
        b_o = o4[pl.ds(b_start, sub16, stride=4), :]
        b_l = l4[pl.ds(b_start, sub16, stride=4), :]
        c_o = o16[r * sub16:(r + 1) * sub16, :]
        c_l = l16[r * sub16:(r + 1) * sub16, :]
        m = jnp.maximum(jnp.maximum(a_l, b_l), c_l)
        ea = jnp.exp(a_l - m)
        eb = jnp.exp(b_l - m)
        ec = jnp.exp(c_l - m)
        out = (ea * a_o + eb * b_o + ec * c_o) / (ea + eb + ec)
        o_ref[pl.ds(r, sub16, stride=16), :] = out


def _dilated_attention(q, k, v, batch, seq):
    n = q.shape[0]
    pairs = DSA_WIDTH // LANES
    blk = pl.BlockSpec((seq, LANES), lambda b, j: (b, j))
    bf = lambda: pltpu.VMEM((seq, LANES), BF16)
    ff = lambda: pltpu.VMEM((seq, LANES), F32)
    return pl.pallas_call(
        _dsa_kernel,
        grid=(batch, pairs),
        in_specs=[blk, blk, blk],
        out_specs=blk,
        out_shape=jax.ShapeDtypeStruct((n, DSA_WIDTH), F32),
        scratch_shapes=[bf() for _ in range(12)] + [ff() for _ in range(6)],
        compiler_params=_cparams(("parallel", "parallel")),
        name="dilated_attention",
    )(q, k, v)


def _poolmem_kernel(u_ref, qm_ref, mem_ref, wkv_ref, wpool_ref, pscale_ref, y_ref, kv_s, qa_s, qb_s):
    seq = u_ref.shape[0]
    u = u_ref[...]
    rows = lax.broadcasted_iota(jnp.int32, (seq, POOL_WIDTH), 0)
    grp = lax.broadcasted_iota(jnp.int32, (seq, POOL_WIDTH), 1) // POOL_GROUP

    def shifted(a, kk):
        return jnp.where(rows >= kk, pltpu.roll(a, kk, 0), 0.0)

    s2 = u + shifted(u, 1)
    s4 = s2 + shifted(s2, 2)
    s8 = s4 + shifted(s4, 4)
    s16 = s8 + shifted(s8, 8)
    wsum = jnp.where(grp == 0, s2, jnp.where(grp == 1, s4, jnp.where(grp == 2, s8, s16)))
    win = jnp.where(grp == 0, 2, jnp.where(grp == 1, 4, jnp.where(grp == 2, 8, 16)))
    cnt = jnp.minimum(rows + 1, win).astype(F32)
    pooled = wsum / cnt - u
    mixed = jnp.dot(pooled.astype(BF16), wpool_ref[...], preferred_element_type=F32) * pscale_ref[...]
    y_ref[:, :POOL_WIDTH] = mixed

    kv_s[...] = jnp.dot(mem_ref[...].astype(BF16), wkv_ref[...], preferred_element_type=F32).astype(BF16)
    first = lax.broadcasted_iota(jnp.int32, (seq, LANES), 1) < HEAD_DIM
    for j in range(MEM_WIDTH // LANES):
        qv = qm_ref[:, j * LANES:(j + 1) * LANES]
        qa_s[...] = jnp.where(first, qv, 0.0).astype(BF16)
        qb_s[...] = jnp.where(first, 0.0, qv).astype(BF16)
        kk = kv_s[:, j * LANES:(j + 1) * LANES]
        vv = kv_s[:, MEM_WIDTH + j * LANES:MEM_WIDTH + (j + 1) * LANES]

        def body(i, carry):
            qrow = pl.multiple_of(i * QBLK, QBLK)
            q2 = jnp.concatenate([qa_s[pl.ds(qrow, QBLK), :], qb_s[pl.ds(qrow, QBLK), :]], axis=0)
            o, _ = _two_head_attention(q2, kk, vv, None)
            y_ref[pl.ds(qrow, QBLK), pl.ds(POOL_WIDTH + j * LANES, LANES)] = o
            return carry
        lax.fori_loop(0, seq // QBLK, body, 0)


def _pool_and_memory(u, qm, mem2d, wkv_bf, wpool_bd, pscale, batch, seq):
    n = u.shape[0]
    mem_len = mem2d.shape[0] // batch
    row = lambda b: (b, 0)
    fix = lambda b: (0, 0)
    return pl.pallas_call(
        _poolmem_kernel,
        grid=(batch,),
        in_specs=[
            pl.BlockSpec((seq, POOL_WIDTH), row),
            pl.BlockSpec((seq, MEM_WIDTH), row),
            pl.BlockSpec((mem_len, D_MODEL), row),
            pl.BlockSpec((D_MODEL, 2 * MEM_WIDTH), fix),
            pl.BlockSpec((POOL_WIDTH, POOL_WIDTH), fix),
            pl.BlockSpec((1, POOL_WIDTH), fix),
        ],
        out_specs=pl.BlockSpec((seq, POOL_WIDTH + MEM_WIDTH), row),
        out_shape=jax.ShapeDtypeStruct((n, POOL_WIDTH + MEM_WIDTH), F32),
        scratch_shapes=[
            pltpu.VMEM((mem_len, 2 * MEM_WIDTH), BF16),
            pltpu.VMEM((seq, LANES), BF16),
            pltpu.VMEM((seq, LANES), BF16),
        ],
        compiler_params=_cparams(("parallel",)),
        name="pool_memory",
    )(u, qm, mem2d, wkv_bf, wpool_bd, pscale)


def _layer_norm(z, g, b):
    mu = jnp.mean(z, axis=-1, keepdims=True)
    zc = z - mu
    var = jnp.mean(zc * zc, axis=-1, keepdims=True)
    return zc * lax.rsqrt(var + LN_EPS) * g + b


def _store_token_rows(dst_ref, val):
    tm = val.shape[0]
    for s in range(ROW_VREGS):
        dst_ref[pl.ds(s, tm, stride=ROW_VREGS), :] = val[:, s * LANES:(s + 1) * LANES]


def _load_token_rows(src_ref, tm):
    return jnp.concatenate(
        [src_ref[pl.ds(s, tm, stride=ROW_VREGS), :] for s in range(ROW_VREGS)], axis=1)


def _route(x1, wrh, wrl, rbias, ridx_ref, rw_ref):
    tm = x1.shape[0]
    xh = x1.astype(BF16)
    xl = (x1 - xh.astype(F32)).astype(BF16)
    nt = (((1,), (1,)), ((), ()))
    logits = (lax.dot_general(wrh, xh, nt, preferred_element_type=F32)
              + lax.dot_general(wrh, xl, nt, preferred_element_type=F32)
              + lax.dot_general(wrl, xh, nt, preferred_element_type=F32))
    scores = jax.nn.sigmoid(logits)
    biased = scores + rbias

    per = N_EXPERTS // N_GROUPS
    io8 = lax.broadcasted_iota(jnp.int32, (per, tm), 0)
    gscores = []
    for g in range(N_GROUPS):
        vg = biased[g * per:(g + 1) * per]
        m1 = jnp.max(vg, axis=0, keepdims=True)
        i1 = jnp.min(jnp.where(vg == m1, io8, per), axis=0, keepdims=True)
        m2 = jnp.max(jnp.where(io8 == i1, -jnp.inf, vg), axis=0, keepdims=True)
        gscores.append(m1 + m2)
    gs = jnp.concatenate(gscores, axis=0)
    iog = lax.broadcasted_iota(jnp.int32, (N_GROUPS, tm), 0)
    grank = jnp.zeros((N_GROUPS, tm), F32)
    for g in range(N_GROUPS):
        sg = gs[g:g + 1]
        tie = jnp.where(iog > g, 1.0, 0.0)
        grank = grank + jnp.where(sg > gs, 1.0, jnp.where(sg == gs, tie, 0.0))
    gsel = grank < TOPK_GROUPS
    masked = jnp.concatenate(
        [jnp.where(gsel[g:g + 1], biased[g * per:(g + 1) * per], -jnp.inf) for g in range(N_GROUPS)], axis=0)

    ioe = lax.broadcasted_iota(jnp.int32, (N_EXPERTS, tm), 0)
    rank = jnp.zeros((N_EXPERTS, tm), F32)
    for e in range(N_EXPERTS):
        me = masked[e:e + 1]
        tie = jnp.where(ioe > e, 1.0, 0.0)
        rank = rank + jnp.where(me > masked, 1.0, jnp.where(me == masked, tie, 0.0))

    idx_rows, w_rows = [], []
    for kk in range(TOP_K):
        hit = rank == float(kk)
        idx_rows.append(jnp.sum(jnp.where(hit, ioe, 0), axis=0, keepdims=True))
        w_rows.append(jnp.sum(jnp.where(hit, scores, 0.0), axis=0, keepdims=True))
    top_w = jnp.concatenate(w_rows, axis=0)
    denom = jnp.sum(top_w, axis=0, keepdims=True)
    ridx_ref[...] = jnp.concatenate(idx_rows, axis=0)
    rw_ref[...] = top_w / denom * ROUTED_SCALE


def _outproj_kernel(alpha, x_ref, ypm_ref, ydsa_ref, wo_ref, g_ref, b_ref, wrh_ref, wrl_ref, rb_ref,
                    x1_ref, x1g_ref, ridx_ref, rw_ref):
    y = jnp.concatenate([ypm_ref[:, :POOL_WIDTH], ydsa_ref[...], ypm_ref[:, POOL_WIDTH:]], axis=1)
    mix = jnp.dot(y.astype(BF16), wo_ref[...], preferred_element_type=F32)
    x1 = _layer_norm(alpha * x_ref[...] + mix, g_ref[...], b_ref[...])
    x1_ref[...] = x1
    _store_token_rows(x1g_ref, x1)
    _route(x1, wrh_ref[...], wrl_ref[...], rb_ref[...], ridx_ref, rw_ref)


def _out_projection(x2d, ypm, ydsa, wo_bf, g, b, wrh, wrl, rbias, alpha):
    n = x2d.shape[0]
    tm = TM_PROJ
    row = lambda i: (i, 0)
    fix = lambda i: (0, 0)
    col = lambda i: (0, i)
    return pl.pallas_call(
        functools.partial(_outproj_kernel, alpha),
        grid=(n // tm,),
        in_specs=[
            pl.BlockSpec((tm, D_MODEL), row),
            pl.BlockSpec((tm, POOL_WIDTH + MEM_WIDTH), row),
            pl.BlockSpec((tm, DSA_WIDTH), row),
            pl.BlockSpec((D_MODEL, D_MODEL), fix),
            pl.BlockSpec((1, D_MODEL), fix),
            pl.BlockSpec((1, D_MODEL), fix),
            pl.BlockSpec((N_EXPERTS, D_MODEL), fix),
            pl.BlockSpec((N_EXPERTS, D_MODEL), fix),
            pl.BlockSpec((N_EXPERTS, 1), fix),
        ],
        out_specs=[
            pl.BlockSpec((tm, D_MODEL), row),
            pl.BlockSpec((tm * ROW_VREGS, LANES), row),
            pl.BlockSpec((TOP_K, tm), col),
            pl.BlockSpec((TOP_K, tm), col),
        ],
        out_shape=[
            jax.ShapeDtypeStruct((n, D_MODEL), F32),
            jax.ShapeDtypeStruct((n * ROW_VREGS, LANES), F32),
            jax.ShapeDtypeStruct((TOP_K, n), jnp.int32),
            jax.ShapeDtypeStruct((TOP_K, n), F32),
        ],
        compiler_params=_cparams(("parallel",)),
        name="out_projection_router",
    )(x2d, ypm, ydsa, wo_bf, g, b, wrh, wrl, rbias)


def _moe_kernel(chunk, cnt_ref, off_ref, tok_ref, wl_ref, xg_ref, wg_ref, wu_ref, wd_ref, acc_ref,
                rows_s, yrows_s):
    c = pl.program_id(0)
    e = pl.program_id(1)
    list_len = tok_ref.shape[0]
    dummy = chunk

    @pl.when(e == 0)
    def _():
        acc_ref[...] = jnp.zeros(acc_ref.shape, F32)

    @pl.when(jnp.logical_and(c == 0, e == 0))
    def _():
        rows_s[...] = jnp.zeros(rows_s.shape, F32)

    n = cnt_ref[c * N_EXPERTS + e]
    off = off_ref[c * N_EXPERTS + e]
    wg = wg_ref[0].astype(BF16)
    wu = wu_ref[0].astype(BF16)
    wd = wd_ref[0].astype(BF16)
    n_blk = (n + MOE_BM - 1) // MOE_BM

    def block(bi, carry):
        base = off + bi * MOE_BM
        n_rows = jnp.minimum(n - bi * MOE_BM, MOE_BM)
        n_grp = (n_rows + SUBLANES - 1) // SUBLANES

        def gather(g, cc):
            for i in range(SUBLANES):
                r = g * SUBLANES + i
                tok = tok_ref[jnp.minimum(base + r, list_len - 1)]
                src = pl.multiple_of(tok * ROW_VREGS, ROW_VREGS)
                dst = pl.multiple_of(r * ROW_VREGS, ROW_VREGS)
                rows_s[pl.ds(dst, ROW_VREGS), :] = xg_ref[pl.ds(src, ROW_VREGS), :]
            return cc
        lax.fori_loop(0, n_grp, gather, 0)

        xb = _load_token_rows(rows_s, MOE_BM).astype(BF16)
        gg = jnp.dot(xb, wg, preferred_element_type=F32)
        uu = jnp.dot(xb, wu, preferred_element_type=F32)
        hh = (gg * jax.nn.sigmoid(gg) * uu).astype(BF16)
        yy = jnp.dot(hh, wd, preferred_element_type=F32)
        _store_token_rows(yrows_s, yy)

        def scatter(g, cc):
            upd = []
            for i in range(SUBLANES):
                r = g * SUBLANES + i
                li = jnp.minimum(base + r, list_len - 1)
                tok = jnp.where(r < n_rows, tok_ref[li], dummy)
                dst = pl.multiple_of(tok * ROW_VREGS, ROW_VREGS)
                src = pl.multiple_of(r * ROW_VREGS, ROW_VREGS)
                upd.append((dst, acc_ref[pl.ds(dst, ROW_VREGS), :]
                            + wl_ref[li] * yrows_s[pl.ds(src, ROW_VREGS), :]))
            for dst, val in upd:
                acc_ref[pl.ds(dst, ROW_VREGS), :] = val
            return cc
        lax.fori_loop(0, n_grp, scatter, 0)
        return carry

    lax.fori_loop(0, n_blk, block, 0)


def _routed_experts(x1g, tok_list, w_list, counts, offsets, w_gate, w_up, w_down, chunk):
    n_tok = x1g.shape[0] // ROW_VREGS
    n_chunks = n_tok // chunk
    list_len = chunk * TOP_K
    acc_rows = (chunk + SUBLANES) * ROW_VREGS
    grid_spec = pltpu.PrefetchScalarGridSpec(
        num_scalar_prefetch=2,
        grid=(n_chunks, N_EXPERTS),
        in_specs=[
            pl.BlockSpec((list_len,), lambda c, e, *_: (c,), memory_space=pltpu.SMEM),
            pl.BlockSpec((list_len,), lambda c, e, *_: (c,), memory_space=pltpu.SMEM),
            pl.BlockSpec((chunk * ROW_VREGS, LANES), lambda c, e, *_: (c, 0), pipeline_mode=pl.Buffered(1)),
            pl.BlockSpec((1, D_MODEL, D_EXPERT), lambda c, e, *_: (e, 0, 0)),
            pl.BlockSpec((1, D_MODEL, D_EXPERT), lambda c, e, *_: (e, 0, 0)),
            pl.BlockSpec((1, D_EXPERT, D_MODEL), lambda c, e, *_: (e, 0, 0)),
        ],
        out_specs=pl.BlockSpec((None, acc_rows, LANES), lambda c, e, *_: (c, 0, 0), pipeline_mode=pl.Buffered(1)),
        scratch_shapes=[
            pltpu.VMEM((MOE_BM * ROW_VREGS, LANES), F32),
            pltpu.VMEM((MOE_BM * ROW_VREGS, LANES), F32),
        ],
    )
    return pl.pallas_call(
        functools.partial(_moe_kernel, chunk),
        grid_spec=grid_spec,
        out_shape=jax.ShapeDtypeStruct((n_chunks, acc_rows, LANES), F32),
        compiler_params=_cparams(("arbitrary", "arbitrary")),
        name="routed_experts",
    )(counts, offsets, tok_list, w_list, x1g, w_gate, w_up, w_down)


def _shared_kernel(alpha, x_ref, acc_ref, wg_ref, wu_ref, wd_ref, g_ref, b_ref, o_ref):
    x1 = x_ref[...]
    tm = x1.shape[0]
    xb = x1.astype(BF16)
    gg = jnp.dot(xb, wg_ref[...], preferred_element_type=F32)
    uu = jnp.dot(xb, wu_ref[...], preferred_element_type=F32)
    hh = (gg * jax.nn.sigmoid(gg) * uu).astype(BF16)
    shared = jnp.dot(hh, wd_ref[...], preferred_element_type=F32)
    routed = _load_token_rows(acc_ref, tm)
    o_ref[...] = _layer_norm(alpha * x1 + (routed + shared), g_ref[...], b_ref[...])


def _shared_and_norm(x1, acc, wsg, wsu, wsd, g, b, alpha, chunk):
    n = x1.shape[0]
    tm = TM_PROJ
    per_chunk = chunk // tm
    row = lambda i: (i, 0)
    fix = lambda i: (0, 0)
    return pl.pallas_call(
        functools.partial(_shared_kernel, alpha),
        grid=(n // tm,),
        in_specs=[
            pl.BlockSpec((tm, D_MODEL), row),
            pl.BlockSpec((None, tm * ROW_VREGS, LANES), lambda i: (i // per_chunk, i % per_chunk, 0)),
            pl.BlockSpec((D_MODEL, D_EXPERT), fix),
            pl.BlockSpec((D_MODEL, D_EXPERT), fix),
            pl.BlockSpec((D_EXPERT, D_MODEL), fix),
            pl.BlockSpec((1, D_MODEL), fix),
            pl.BlockSpec((1, D_MODEL), fix),
        ],
        out_specs=pl.BlockSpec((tm, D_MODEL), row),
        out_shape=jax.ShapeDtypeStruct((n, D_MODEL), F32),
        compiler_params=_cparams(("parallel",)),
        name="shared_expert_norm",
    )(x1, acc, wsg, wsu, wsd, g, b)


def _rope_tables(seq):
    half = ROT_DIM // 2
    pos = jnp.arange(seq, dtype=F32)
    inv = ROPE_THETA ** (-jnp.arange(0, ROT_DIM, 2, dtype=F32) / ROT_DIM)
    ang = pos[:, None] * inv[None, :]
    cos, sin = jnp.cos(ang), jnp.sin(ang)
    pad = HEAD_DIM - ROT_DIM
    ones = jnp.ones((seq, pad), F32)
    zeros = jnp.zeros((seq, pad), F32)
    zh = jnp.zeros((seq, half), F32)
    c_head = jnp.concatenate([cos, cos, ones], axis=1)
    s1_head = jnp.concatenate([-sin, zh, zeros], axis=1)
    s2_head = jnp.concatenate([zh, sin, zeros], axis=1)
    rep = LANES // HEAD_DIM
    return (jnp.tile(c_head, (1, rep)), jnp.tile(s1_head, (1, rep)), jnp.tile(s2_head, (1, rep)))


def _dispatch_lists(ridx, rw, chunk):
    n = ridx.shape[1]
    n_chunks = n // chunk
    e_flat = ridx.T.reshape(n_chunks, chunk * TOP_K)
    w_flat = rw.T.reshape(n_chunks, chunk * TOP_K)
    pos = lax.broadcasted_iota(jnp.int32, e_flat.shape, 1)
    e_sorted, order = lax.sort((e_flat, pos), dimension=1, is_stable=True, num_keys=1)
    tok = order // TOP_K
    w_sorted = jnp.take_along_axis(w_flat, order, axis=1)
    experts = jnp.arange(N_EXPERTS, dtype=jnp.int32)
    starts = jax.vmap(lambda row: jnp.searchsorted(row, experts, side="left"))(e_sorted).astype(jnp.int32)
    ends = jax.vmap(lambda row: jnp.searchsorted(row, experts, side="right"))(e_sorted).astype(jnp.int32)
    return (tok.reshape(-1).astype(jnp.int32), w_sorted.reshape(-1),
            (ends - starts).reshape(-1), starts.reshape(-1))


def kernel(x, mem, w_in, w_pool, pool_scale, w_mem_kv, w_out, ln1_g, ln1_b, w_router, router_bias,
           w_gate, w_up, w_down, ws_gate, ws_up, ws_down, ln2_g, ln2_b):
    batch, seq, d = x.shape
    depth = w_in.shape[0]
    assert d == D_MODEL and seq % (16 * QBLK) == 0 and seq % TM_PROJ == 0
    n = batch * seq
    chunk = min(MOE_CHUNK, n)
    assert n % chunk == 0 and chunk % TM_PROJ == 0
    alpha = float((2 * depth) ** 0.25)

    rope_c, rope_s1, rope_s2 = _rope_tables(seq)
    x2d = x.reshape(n, d)
    mem2d = mem.reshape(batch * mem.shape[1], d)

    for l in range(depth):
        w_in_bf = w_in[l].astype(BF16)
        wkv_bf = w_mem_kv[l].astype(BF16)
        wo_bf = w_out[l].astype(BF16)
        wpool_bd = jax.scipy.linalg.block_diag(*[w_pool[l, g] for g in range(len(POOL_WINDOWS))]).astype(BF16)
        pscale = pool_scale[l].reshape(1, POOL_WIDTH)
        wr_t = w_router[l].T
        wrh = wr_t.astype(BF16)
        wrl = (wr_t - wrh.astype(F32)).astype(BF16)
        rbias = router_bias[l].reshape(N_EXPERTS, 1)

        u, q, k, v, qm = _in_projection(x2d, w_in_bf, rope_c, rope_s1, rope_s2, seq)
        ydsa = _dilated_attention(q, k, v, batch, seq)
        ypm = _pool_and_memory(u, qm, mem2d, wkv_bf, wpool_bd, pscale, batch, seq)
        x1, x1g, ridx, rw = _out_projection(
            x2d, ypm, ydsa, wo_bf, ln1_g[l].reshape(1, d), ln1_b[l].reshape(1, d), wrh, wrl, rbias, alpha)
        tok_list, w_list, counts, offsets = _dispatch_lists(ridx, rw, chunk)
        acc = _routed_experts(x1g, tok_list, w_list, counts, offsets, w_gate[l], w_up[l], w_down[l], chunk)
        x2d = _shared_and_norm(
            x1, acc, ws_gate[l].astype(BF16), ws_up[l].astype(BF16), ws_down[l].astype(BF16),
            ln2_g[l].reshape(1, d), ln2_b[l].reshape(1, d), alpha, chunk)
    return x2d.reshape(batch, seq, d)
```

```python
import functools

import jax
import jax.numpy as jnp
from jax import lax
from jax.experimental import pallas as pl
from jax.experimental.pallas import tpu as pltpu

F32 = jnp.float32
BF16 = jnp.bfloat16

D_MODEL = 1024
HEAD_DIM = 64
POOL_WINDOWS = (2, 4, 8, 16)
POOL_WIDTH = 256
POOL_GROUP = 64
DSA_WIDTH = 512
DSA_LOOKBACK = 128
DSA_DILATIONS = (1, 4, 16)
MEM_WIDTH = 256
IN_WIDTH = 2048
ROPE_THETA = 500000.0
ROT_DIM = 16
N_EXPERTS = 64
TOP_K = 8
N_GROUPS = 8
TOPK_GROUPS = 4
D_EXPERT = 256
ROUTED_SCALE = 2.5
LN_EPS = 1e-5

LANES = 128
SUBLANES = 8
ROW_VREGS = D_MODEL // LANES

NEG_BIG = -1e30

TM_PROJ = 512
QBLK = 128
MOE_CHUNK = 4096
MOE_BM = 256
MOE_UNROLL = 8
VMEM_LIMIT = 56 * 1024 * 1024


def _cparams(sem):
    return pltpu.CompilerParams(dimension_semantics=sem, vmem_limit_bytes=VMEM_LIMIT)


def _inproj_kernel(x_ref, w_ref, c_ref, s1_ref, s2_ref, u_ref, q_ref, k_ref, v_ref, qm_ref):
    xb = x_ref[...].astype(BF16)
    h = jnp.dot(xb, w_ref[...], preferred_element_type=F32)
    c = c_ref[...]
    s1 = s1_ref[...]
    s2 = s2_ref[...]

    def rot(a):
        return a * c + pltpu.roll(a, LANES - ROT_DIM // 2, 1) * s1 + pltpu.roll(a, ROT_DIM // 2, 1) * s2

    scale = HEAD_DIM ** -0.5
    u_ref[...] = h[:, :POOL_WIDTH]
    q0 = POOL_WIDTH
    k0 = q0 + DSA_WIDTH
    v0 = k0 + DSA_WIDTH
    m0 = v0 + DSA_WIDTH
    for j in range(DSA_WIDTH // LANES):
        q_ref[:, j * LANES:(j + 1) * LANES] = rot(h[:, q0 + j * LANES:q0 + (j + 1) * LANES]) * scale
        k_ref[:, j * LANES:(j + 1) * LANES] = rot(h[:, k0 + j * LANES:k0 + (j + 1) * LANES])
    v_ref[...] = h[:, v0:m0]
    qm_ref[...] = h[:, m0:] * scale


def _in_projection(x2d, w_in_bf, rope_c, rope_s1, rope_s2, seq):
    n = x2d.shape[0]
    tm = TM_PROJ
    per_seq = seq // tm
    row = lambda i: (i, 0)
    tab = lambda i: (i % per_seq, 0)
    return pl.pallas_call(
        _inproj_kernel,
        grid=(n // tm,),
        in_specs=[
            pl.BlockSpec((tm, D_MODEL), row),
            pl.BlockSpec((D_MODEL, IN_WIDTH), lambda i: (0, 0)),
            pl.BlockSpec((tm, LANES), tab),
            pl.BlockSpec((tm, LANES), tab),
            pl.BlockSpec((tm, LANES), tab),
        ],
        out_specs=[
            pl.BlockSpec((tm, POOL_WIDTH), row),
            pl.BlockSpec((tm, DSA_WIDTH), row),
            pl.BlockSpec((tm, DSA_WIDTH), row),
            pl.BlockSpec((tm, DSA_WIDTH), row),
            pl.BlockSpec((tm, MEM_WIDTH), row),
        ],
        out_shape=[
            jax.ShapeDtypeStruct((n, POOL_WIDTH), F32),
            jax.ShapeDtypeStruct((n, DSA_WIDTH), F32),
            jax.ShapeDtypeStruct((n, DSA_WIDTH), F32),
            jax.ShapeDtypeStruct((n, DSA_WIDTH), F32),
            jax.ShapeDtypeStruct((n, MEM_WIDTH), F32),
        ],
        compiler_params=_cparams(("parallel",)),
        name="in_projection",
    )(x2d, w_in_bf, rope_c, rope_s1, rope_s2)


def _two_head_attention(q2, kk, vv, bias):
    s = lax.dot_general(q2, kk, (((1,), (1,)), ((), ())), preferred_element_type=F32)
    if bias is not None:
        s = s + bias
    m = jnp.max(s, axis=-1, keepdims=True)
    p = jnp.exp(s - m)
    l = jnp.sum(p, axis=-1, keepdims=True)
    o2 = jnp.dot(p.astype(BF16), vv, preferred_element_type=F32) / l
    lse2 = m + jnp.log(l)
    first = lax.broadcasted_iota(jnp.int32, (QBLK, LANES), 1) < HEAD_DIM
    o = jnp.where(first, o2[:QBLK], o2[QBLK:])
    lse = jnp.where(first, lse2[:QBLK], lse2[QBLK:])
    return o, lse


def _band_bias(nk, dbase):
    qi = lax.broadcasted_iota(jnp.int32, (2 * QBLK, nk), 0) & (QBLK - 1)
    kj = lax.broadcasted_iota(jnp.int32, (2 * QBLK, nk), 1)
    diff = qi - kj + dbase
    ok = lax.bitcast_convert_type(diff, jnp.uint32) <= jnp.uint32(DSA_LOOKBACK)
    return jnp.where(ok, 0.0, NEG_BIG).astype(F32)


def _dsa_kernel(q_ref, k_ref, v_ref, o_ref,
                qa1, qb1, k1, v1, qa4, qb4, k4, v4, qa16, qb16, k16, v16,
                o1, l1, o4, l4, o16, l16):
    seq = q_ref.shape[0]

    def stage(dil, qa, qb, kd, vd):
        sub = seq // dil
        for r in range(dil):
            if dil == 1:
                qv, kv, vv = q_ref[...], k_ref[...], v_ref[...]
            else:
                qv = q_ref[pl.ds(r, sub, stride=dil), :]
                kv = k_ref[pl.ds(r, sub, stride=dil), :]
                vv = v_ref[pl.ds(r, sub, stride=dil), :]
            fm = lax.broadcasted_iota(jnp.int32, (sub, LANES), 1) < HEAD_DIM
            qa[r * sub:(r + 1) * sub, :] = jnp.where(fm, qv, 0.0).astype(BF16)
            qb[r * sub:(r + 1) * sub, :] = jnp.where(fm, 0.0, qv).astype(BF16)
            kd[r * sub:(r + 1) * sub, :] = kv.astype(BF16)
            vd[r * sub:(r + 1) * sub, :] = vv.astype(BF16)

    stage(1, qa1, qb1, k1, v1)
    stage(4, qa4, qb4, k4, v4)
    stage(16, qa16, qb16, k16, v16)

    n_blocks = seq // QBLK

    def run(qa, qb, kd, vd, od, ld, blocks_per_seq, nk):
        def body(i, carry):
            qrow = pl.multiple_of(i * QBLK, QBLK)
            has_prev = jnp.where((i % blocks_per_seq) > 0, 1, 0) if nk > QBLK else 0
            krow = pl.multiple_of((i - has_prev) * QBLK, QBLK)
            q2 = jnp.concatenate([qa[pl.ds(qrow, QBLK), :], qb[pl.ds(qrow, QBLK), :]], axis=0)
            kk = kd[pl.ds(krow, nk), :]
            vv = vd[pl.ds(krow, nk), :]
            o, lse = _two_head_attention(q2, kk, vv, _band_bias(nk, has_prev * QBLK))
            od[pl.ds(qrow, QBLK), :] = o
            ld[pl.ds(qrow, QBLK), :] = lse
            return carry
        lax.fori_loop(0, n_blocks, body, 0)

    run(qa1, qb1, k1, v1, o1, l1, n_blocks, 2 * QBLK)
    run(qa4, qb4, k4, v4, o4, l4, n_blocks // 4, 2 * QBLK)
    run(qa16, qb16, k16, v16, o16, l16, 1, QBLK)

    sub16 = seq // 16
    sub4 = seq // 4
    for r in range(16):
        a_o = o1[pl.ds(r, sub16, stride=16), :]
        a_l = l1[pl.ds(r, sub16, stride=16), :]
        b_start = (r % 4) * sub4 + r // 4
        b_o = o4[pl.ds(b_start, sub16, stride=4), :]
        b_l = l4[pl.ds(b_start, sub16, stride=4), :]
        c_o = o16[r * sub16:(r + 1) * sub16, :]
        c_l = l16[r * sub16:(r + 1) * sub16, :]
        m = jnp.maximum(jnp.maximum(a_l, b_l), c_l)
        ea = jnp.exp(a_l - m)
        eb = jnp.exp(b_l - m)
        ec = jnp.exp(c_l - m)
        out = (ea * a_o + eb * b_o + ec * c_o) / (ea + eb + ec)
        o_ref[pl.ds(r, sub16, stride=16), :] = out


def _dilated_attention(q, k, v, batch, seq):
    n = q.shape[0]
    pairs = DSA_WIDTH // LANES
    blk = pl.BlockSpec((seq, LANES), lambda b, j: (b, j))
    bf = lambda: pltpu.VMEM((seq, LANES), BF16)
    ff = lambda: pltpu.VMEM((seq, LANES), F32)
    return pl.pallas_call(
        _dsa_kernel,
        grid=(batch, pairs),
        in_specs=[blk, blk, blk],
        out_specs=blk,
        out_shape=jax.ShapeDtypeStruct((n, DSA_WIDTH), F32),
        scratch_shapes=[bf() for _ in range(12)] + [ff() for _ in range(6)],
        compiler_params=_cparams(("parallel", "parallel")),
        name="dilated_attention",
    )(q, k, v)


def _poolmem_kernel(u_ref, qm_ref, mem_ref, wkv_ref, wpool_ref, pscale_ref, y_ref, kv_s, qa_s, qb_s):
    seq = u_ref.shape[0]
    u = u_ref[...]
    rows = lax.broadcasted_iota(jnp.int32, (seq, POOL_WIDTH), 0)
    grp = lax.broadcasted_iota(jnp.int32, (seq, POOL_WIDTH), 1) // POOL_GROUP

    def shifted(a, kk):
        return jnp.where(rows >= kk, pltpu.roll(a, kk, 0), 0.0)

    s2 = u + shifted(u, 1)
    s4 = s2 + shifted(s2, 2)
    s8 = s4 + shifted(s4, 4)
    s16 = s8 + shifted(s8, 8)
    wsum = jnp.where(grp == 0, s2, jnp.where(grp == 1, s4, jnp.where(grp == 2, s8, s16)))
    win = jnp.where(grp == 0, 2, jnp.where(grp == 1, 4, jnp.where(grp == 2, 8, 16)))
    cnt = jnp.minimum(rows + 1, win).astype(F32)
    pooled = wsum / cnt - u
    mixed = jnp.dot(pooled.astype(BF16), wpool_ref[...], preferred_element_type=F32) * pscale_ref[...]
    y_ref[:, :POOL_WIDTH] = mixed

    kv_s[...] = jnp.dot(mem_ref[...].astype(BF16), wkv_ref[...], preferred_element_type=F32).astype(BF16)
    first = lax.broadcasted_iota(jnp.int32, (seq, LANES), 1) < HEAD_DIM
    for j in range(MEM_WIDTH // LANES):
        qv = qm_ref[:, j * LANES:(j + 1) * LANES]
        qa_s[...] = jnp.where(first, qv, 0.0).astype(BF16)
        qb_s[...] = jnp.where(first, 0.0, qv).astype(BF16)
        kk = kv_s[:, j * LANES:(j + 1) * LANES]
        vv = kv_s[:, MEM_WIDTH + j * LANES:MEM_WIDTH + (j + 1) * LANES]

        def body(i, carry):
            qrow = pl.multiple_of(i * QBLK, QBLK)
            q2 = jnp.concatenate([qa_s[pl.ds(qrow, QBLK), :], qb_s[pl.ds(qrow, QBLK), :]], axis=0)
            o, _ = _two_head_attention(q2, kk, vv, None)
            y_ref[pl.ds(qrow, QBLK), pl.ds(POOL_WIDTH + j * LANES, LANES)] = o
            return carry
        lax.fori_loop(0, seq // QBLK, body, 0)


def _pool_and_memory(u, qm, mem2d, wkv_bf, wpool_bd, pscale, batch, seq):
    n = u.shape[0]
    mem_len = mem2d.shape[0] // batch
    row = lambda b: (b, 0)
    fix = lambda b: (0, 0)
    return pl.pallas_call(
        _poolmem_kernel,
        grid=(batch,),
        in_specs=[
            pl.BlockSpec((seq, POOL_WIDTH), row),
            pl.BlockSpec((seq, MEM_WIDTH), row),
            pl.BlockSpec((mem_len, D_MODEL), row),
            pl.BlockSpec((D_MODEL, 2 * MEM_WIDTH), fix),
            pl.BlockSpec((POOL_WIDTH, POOL_WIDTH), fix),
            pl.BlockSpec((1, POOL_WIDTH), fix),
        ],
        out_specs=pl.BlockSpec((seq, POOL_WIDTH + MEM_WIDTH), row),
        out_shape=jax.ShapeDtypeStruct((n, POOL_WIDTH + MEM_WIDTH), F32),
        scratch_shapes=[
            pltpu.VMEM((mem_len, 2 * MEM_WIDTH), BF16),
            pltpu.VMEM((seq, LANES), BF16),
            pltpu.VMEM((seq, LANES), BF16),
        ],
        compiler_params=_cparams(("parallel",)),
        name="pool_memory",
    )(u, qm, mem2d, wkv_bf, wpool_bd, pscale)


def _layer_norm(z, g, b):
    mu = jnp.mean(z, axis=-1, keepdims=True)
    zc = z - mu
    var = jnp.mean(zc * zc, axis=-1, keepdims=True)
    return zc * lax.rsqrt(var + LN_EPS) * g + b


def _store_token_rows(dst_ref, val):
    tm = val.shape[0]
    for s in range(ROW_VREGS):
        dst_ref[pl.ds(s, tm, stride=ROW_VREGS), :] = val[:, s * LANES:(s + 1) * LANES]


def _load_token_rows(src_ref, tm):
    return jnp.concatenate(
        [src_ref[pl.ds(s, tm, stride=ROW_VREGS), :] for s in range(ROW_VREGS)], axis=1)


def _route(x1, wrh, wrl, rbias, ridx_ref, rw_ref):
    tm = x1.shape[0]
    xh = x1.astype(BF16)
    xl = (x1 - xh.astype(F32)).astype(BF16)
    nt = (((1,), (1,)), ((), ()))
    logits = (lax.dot_general(wrh, xh, nt, preferred_element_type=F32)
              + lax.dot_general(wrh, xl, nt, preferred_element_type=F32)
              + lax.dot_general(wrl, xh, nt, preferred_element_type=F32))
    scores = jax.nn.sigmoid(logits)
    biased = scores + rbias

    per = N_EXPERTS // N_GROUPS
    io8 = lax.broadcasted_iota(jnp.int32, (per, tm), 0)
    gscores = []
    for g in range(N_GROUPS):
        vg = biased[g * per:(g + 1) * per]
        m1 = jnp.max(vg, axis=0, keepdims=True)
        i1 = jnp.min(jnp.where(vg == m1, io8, per), axis=0, keepdims=True)
        m2 = jnp.max(jnp.where(io8 == i1, -jnp.inf, vg), axis=0, keepdims=True)
        gscores.append(m1 + m2)
    gs = jnp.concatenate(gscores, axis=0)
    iog = lax.broadcasted_iota(jnp.int32, (N_GROUPS, tm), 0)
    grank = jnp.zeros((N_GROUPS, tm), F32)
    for g in range(N_GROUPS):
        sg = gs[g:g + 1]
        tie = jnp.where(iog > g, 1.0, 0.0)
        grank = grank + jnp.where(sg > gs, 1.0, jnp.where(sg == gs, tie, 0.0))
    gsel = grank < TOPK_GROUPS
    masked = jnp.concatenate(
        [jnp.where(gsel[g:g + 1], biased[g * per:(g + 1) * per], -jnp.inf) for g in range(N_GROUPS)], axis=0)

    ioe = lax.broadcasted_iota(jnp.int32, (N_EXPERTS, tm), 0)
    rank = jnp.zeros((N_EXPERTS, tm), F32)
    for e in range(N_EXPERTS):
        me = masked[e:e + 1]
        tie = jnp.where(ioe > e, 1.0, 0.0)
        rank = rank + jnp.where(me > masked, 1.0, jnp.where(me == masked, tie, 0.0))

    idx_rows, w_rows = [], []
    for kk in range(TOP_K):
        hit = rank == float(kk)
        idx_rows.append(jnp.sum(jnp.where(hit, ioe, 0), axis=0, keepdims=True))
        w_rows.append(jnp.sum(jnp.where(hit, scores, 0.0), axis=0, keepdims=True))
    top_w = jnp.concatenate(w_rows, axis=0)
    denom = jnp.sum(top_w, axis=0, keepdims=True)
    ridx_ref[...] = jnp.concatenate(idx_rows, axis=0)
    rw_ref[...] = top_w / denom * ROUTED_SCALE


def _outproj_kernel(alpha, x_ref, ypm_ref, ydsa_ref, wo_ref, g_ref, b_ref, wrh_ref, wrl_ref, rb_ref,
                    x1_ref, x1g_ref, ridx_ref, rw_ref):
    y = jnp.concatenate([ypm_ref[:, :POOL_WIDTH], ydsa_ref[...], ypm_ref[:, POOL_WIDTH:]], axis=1)
    mix = jnp.dot(y.astype(BF16), wo_ref[...], preferred_element_type=F32)
    x1 = _layer_norm(alpha * x_ref[...] + mix, g_ref[...], b_ref[...])
    x1_ref[...] = x1
    _store_token_rows(x1g_ref, x1)
    _route(x1, wrh_ref[...], wrl_ref[...], rb_ref[...], ridx_ref, rw_ref)


def _out_projection(x2d, ypm, ydsa, wo_bf, g, b, wrh, wrl, rbias, alpha):
    n = x2d.shape[0]
    tm = TM_PROJ
    row = lambda i: (i, 0)
    fix = lambda i: (0, 0)
    col = lambda i: (0, i)
    return pl.pallas_call(
        functools.partial(_outproj_kernel, alpha),
        grid=(n // tm,),
        in_specs=[
            pl.BlockSpec((tm, D_MODEL), row),
            pl.BlockSpec((tm, POOL_WIDTH + MEM_WIDTH), row),
            pl.BlockSpec((tm, DSA_WIDTH), row),
            pl.BlockSpec((D_MODEL, D_MODEL), fix),
            pl.BlockSpec((1, D_MODEL), fix),
            pl.BlockSpec((1, D_MODEL), fix),
            pl.BlockSpec((N_EXPERTS, D_MODEL), fix),
            pl.BlockSpec((N_EXPERTS, D_MODEL), fix),
            pl.BlockSpec((N_EXPERTS, 1), fix),
        ],
        out_specs=[
            pl.BlockSpec((tm, D_MODEL), row),
            pl.BlockSpec((tm * ROW_VREGS, LANES), row),
            pl.BlockSpec((TOP_K, tm), col),
            pl.BlockSpec((TOP_K, tm), col),
        ],
        out_shape=[
            jax.ShapeDtypeStruct((n, D_MODEL), F32),
            jax.ShapeDtypeStruct((n * ROW_VREGS, LANES), F32),
            jax.ShapeDtypeStruct((TOP_K, n), jnp.int32),
            jax.ShapeDtypeStruct((TOP_K, n), F32),
        ],
        compiler_params=_cparams(("parallel",)),
        name="out_projection_router",
    )(x2d, ypm, ydsa, wo_bf, g, b, wrh, wrl, rbias)


def _moe_blocks(chunk):
    return chunk * TOP_K // MOE_BM + N_EXPERTS


def _moe_kernel(eb_ref, nact_ref, tg_ref, ts_ref, ws_ref, xg_ref, wg_ref, wu_ref, wd_ref, acc_ref,
                rows0, rows1, y0, y1):
    c = pl.program_id(0)
    j = pl.program_id(1)

    @pl.when(j == 0)
    def _():
        acc_ref[...] = jnp.zeros(acc_ref.shape, F32)

    @pl.when(jnp.logical_and(c == 0, j == 0))
    def _():
        for buf in (rows0, rows1, y0, y1):
            buf[...] = jnp.zeros(buf.shape, F32)

    def step(rows_w, rows_r, y_w, y_r):
        for g0 in range(0, MOE_BM, MOE_UNROLL):
            upd = []
            for i in range(g0, g0 + MOE_UNROLL):
                dst = pl.multiple_of(ts_ref[0, 0, i], ROW_VREGS)
                upd.append((dst, acc_ref[pl.ds(dst, ROW_VREGS), :]
                            + ws_ref[0, 0, i] * y_r[i * ROW_VREGS:(i + 1) * ROW_VREGS, :]))
            for dst, val in upd:
                acc_ref[pl.ds(dst, ROW_VREGS), :] = val

        xb = _load_token_rows(rows_r, MOE_BM).astype(BF16)
        gg = jnp.dot(xb, wg_ref[...].astype(BF16), preferred_element_type=F32)
        uu = jnp.dot(xb, wu_ref[...].astype(BF16), preferred_element_type=F32)
        hh = (gg * jax.nn.sigmoid(gg) * uu).astype(BF16)
        yy = jnp.dot(hh, wd_ref[...].astype(BF16), preferred_element_type=F32)
        _store_token_rows(y_w, yy)

        for i in range(MOE_BM):
            src = pl.multiple_of(tg_ref[0, 0, i], ROW_VREGS)
            rows_w[i * ROW_VREGS:(i + 1) * ROW_VREGS, :] = xg_ref[pl.ds(src, ROW_VREGS), :]

    active = j < nact_ref[c] + 2

    @pl.when(jnp.logical_and(active, j % 2 == 0))
    def _():
        step(rows0, rows1, y1, y0)

    @pl.when(jnp.logical_and(active, j % 2 == 1))
    def _():
        step(rows1, rows0, y0, y1)


def _routed_experts(x1g, tok_g, tok_s, w_s, block_expert, n_active, w_gate, w_up, w_down, layer, chunk):
    n_tok = x1g.shape[0] // ROW_VREGS
    n_chunks = n_tok // chunk
    nb = _moe_blocks(chunk)
    acc_rows = (chunk + SUBLANES) * ROW_VREGS

    def gather_list(c, j, eb, na):
        return (c * (nb + 1) + jnp.minimum(j, nb), 0, 0)

    def scatter_list(c, j, eb, na):
        return (c * (nb + 1) + jnp.where(j >= 2, jnp.minimum(j - 2, nb), nb), 0, 0)

    def expert_w(c, j, eb, na):
        return (layer, eb[c * nb + jnp.clip(j - 1, 0, nb - 1)], 0, 0)

    smem_list = lambda imap: pl.BlockSpec((1, 1, MOE_BM), imap, memory_space=pltpu.SMEM)
    buf = lambda: pltpu.VMEM((MOE_BM * ROW_VREGS, LANES), F32)
    grid_spec = pltpu.PrefetchScalarGridSpec(
        num_scalar_prefetch=2,
        grid=(n_chunks, nb + 2),
        in_specs=[
            smem_list(gather_list),
            smem_list(scatter_list),
            smem_list(scatter_list),
            pl.BlockSpec((chunk * ROW_VREGS, LANES), lambda c, j, eb, na: (c, 0), pipeline_mode=pl.Buffered(1)),
            pl.BlockSpec((None, None, D_MODEL, D_EXPERT), expert_w),
            pl.BlockSpec((None, None, D_MODEL, D_EXPERT), expert_w),
            pl.BlockSpec((None, None, D_EXPERT, D_MODEL), expert_w),
        ],
        out_specs=pl.BlockSpec((None, acc_rows, LANES), lambda c, j, eb, na: (c, 0, 0),
                               pipeline_mode=pl.Buffered(1)),
        scratch_shapes=[buf(), buf(), buf(), buf()],
    )
    return pl.pallas_call(
        _moe_kernel,
        grid_spec=grid_spec,
        out_shape=jax.ShapeDtypeStruct((n_chunks, acc_rows, LANES), F32),
        compiler_params=_cparams(("arbitrary", "arbitrary")),
        name="routed_experts",
    )(block_expert, n_active, tok_g, tok_s, w_s, x1g, w_gate, w_up, w_down)


def _shared_kernel(alpha, x_ref, acc_ref, wg_ref, wu_ref, wd_ref, g_ref, b_ref, o_ref):
    x1 = x_ref[...]
    tm = x1.shape[0]
    xb = x1.astype(BF16)
    gg = jnp.dot(xb, wg_ref[...], preferred_element_type=F32)
    uu = jnp.dot(xb, wu_ref[...], preferred_element_type=F32)
    hh = (gg * jax.nn.sigmoid(gg) * uu).astype(BF16)
    shared = jnp.dot(hh, wd_ref[...], preferred_element_type=F32)
    routed = _load_token_rows(acc_ref, tm)
    o_ref[...] = _layer_norm(alpha * x1 + (routed + shared), g_ref[...], b_ref[...])


def _shared_and_norm(x1, acc, wsg, wsu, wsd, g, b, alpha, chunk):
    n = x1.shape[0]
    tm = TM_PROJ
    per_chunk = chunk // tm
    row = lambda i: (i, 0)
    fix = lambda i: (0, 0)
    return pl.pallas_call(
        functools.partial(_shared_kernel, alpha),
        grid=(n // tm,),
        in_specs=[
            pl.BlockSpec((tm, D_MODEL), row),
            pl.BlockSpec((None, tm * ROW_VREGS, LANES), lambda i: (i // per_chunk, i % per_chunk, 0)),
            pl.BlockSpec((D_MODEL, D_EXPERT), fix),
            pl.BlockSpec((D_MODEL, D_EXPERT), fix),
            pl.BlockSpec((D_EXPERT, D_MODEL), fix),
            pl.BlockSpec((1, D_MODEL), fix),
            pl.BlockSpec((1, D_MODEL), fix),
        ],
        out_specs=pl.BlockSpec((tm, D_MODEL), row),
        out_shape=jax.ShapeDtypeStruct((n, D_MODEL), F32),
        compiler_params=_cparams(("parallel",)),
        name="shared_expert_norm",
    )(x1, acc, wsg, wsu, wsd, g, b)


def _rope_tables(seq):
    half = ROT_DIM // 2
    pos = jnp.arange(seq, dtype=F32)
    inv = ROPE_THETA ** (-jnp.arange(0, ROT_DIM, 2, dtype=F32) / ROT_DIM)
    ang = pos[:, None] * inv[None, :]
    cos, sin = jnp.cos(ang), jnp.sin(ang)
    pad = HEAD_DIM - ROT_DIM
    ones = jnp.ones((seq, pad), F32)
    zeros = jnp.zeros((seq, pad), F32)
    zh = jnp.zeros((seq, half), F32)
    c_head = jnp.concatenate([cos, cos, ones], axis=1)
    s1_head = jnp.concatenate([-sin, zh, zeros], axis=1)
    s2_head = jnp.concatenate([zh, sin, zeros], axis=1)
    rep = LANES // HEAD_DIM
    return (jnp.tile(c_head, (1, rep)), jnp.tile(s1_head, (1, rep)), jnp.tile(s2_head, (1, rep)))


def _dispatch_lists(ridx, rw, chunk):
    n = ridx.shape[1]
    n_chunks = n // chunk
    e_flat = ridx.T.reshape(n_chunks, chunk * TOP_K)
    w_flat = rw.T.reshape(n_chunks, chunk * TOP_K)
    n_asg = chunk * TOP_K
    nb = _moe_blocks(chunk)
    pos = lax.broadcasted_iota(jnp.int32, e_flat.shape, 1)
    _, order = lax.sort((e_flat, pos), dimension=1, is_stable=True, num_keys=1)
    experts = jnp.arange(N_EXPERTS, dtype=jnp.int32)
    counts = jnp.sum((e_flat[:, :, None] == experts).astype(jnp.int32), axis=1)
    start = jnp.cumsum(counts, axis=1) - counts
    n_blk = (counts + MOE_BM - 1) // MOE_BM
    b_end = jnp.cumsum(n_blk, axis=1)
    b_start = b_end - n_blk
    n_active = b_end[:, -1]

    jb = jnp.arange(nb + 1, dtype=jnp.int32)
    eb = jnp.minimum(jnp.sum((jb[None, :, None] >= b_end[:, None, :]).astype(jnp.int32), axis=2), N_EXPERTS - 1)
    take = lambda a: jnp.take_along_axis(a, eb, axis=1)
    row = (jb[None, :] - take(b_start))[:, :, None] * MOE_BM + jnp.arange(MOE_BM, dtype=jnp.int32)
    valid = (jb[None, :, None] < n_active[:, None, None]) & (row < take(counts)[:, :, None])
    src = jnp.clip(take(start)[:, :, None] + row, 0, n_asg - 1).reshape(n_chunks, -1)
    asg = jnp.take_along_axis(order, src, axis=1)
    tok = (asg // TOP_K).reshape(valid.shape)
    wgt = jnp.take_along_axis(w_flat, asg, axis=1).reshape(valid.shape)
    shape = (n_chunks * (nb + 1), 1, MOE_BM)
    tok_g = (jnp.where(valid, tok, 0) * ROW_VREGS).reshape(shape)
    tok_s = (jnp.where(valid, tok, chunk) * ROW_VREGS).reshape(shape)
    w_s = jnp.where(valid, wgt, 0.0).reshape(shape)
    return tok_g, tok_s, w_s, eb[:, :nb].reshape(-1), n_active


def kernel(x, mem, w_in, w_pool, pool_scale, w_mem_kv, w_out, ln1_g, ln1_b, w_router, router_bias,
           w_gate, w_up, w_down, ws_gate, ws_up, ws_down, ln2_g, ln2_b):
    batch, seq, d = x.shape
    depth = w_in.shape[0]
    assert d == D_MODEL and seq % (16 * QBLK) == 0 and seq % TM_PROJ == 0
    n = batch * seq
    chunk = min(MOE_CHUNK, n)
    assert n % chunk == 0 and chunk % TM_PROJ == 0
    alpha = float((2 * depth) ** 0.25)

    rope_c, rope_s1, rope_s2 = _rope_tables(seq)
    x2d = x.reshape(n, d)
    mem2d = mem.reshape(batch * mem.shape[1], d)

    for l in range(depth):
        w_in_bf = w_in[l].astype(BF16)
        wkv_bf = w_mem_kv[l].astype(BF16)
        wo_bf = w_out[l].astype(BF16)
        wpool_bd = jax.scipy.linalg.block_diag(*[w_pool[l, g] for g in range(len(POOL_WINDOWS))]).astype(BF16)
        pscale = pool_scale[l].reshape(1, POOL_WIDTH)
        wr_t = w_router[l].T
        wrh = wr_t.astype(BF16)
        wrl = (wr_t - wrh.astype(F32)).astype(BF16)
        rbias = router_bias[l].reshape(N_EXPERTS, 1)

        u, q, k, v, qm = _in_projection(x2d, w_in_bf, rope_c, rope_s1, rope_s2, seq)
        ydsa = _dilated_attention(q, k, v, batch, seq)
        ypm = _pool_and_memory(u, qm, mem2d, wkv_bf, wpool_bd, pscale, batch, seq)
        x1, x1g, ridx, rw = _out_projection(
            x2d, ypm, ydsa, wo_bf, ln1_g[l].reshape(1, d), ln1_b[l].reshape(1, d), wrh, wrl, rbias, alpha)
        tok_g, tok_s, w_s, block_expert, n_active = _dispatch_lists(ridx, rw, chunk)
        acc = _routed_experts(x1g, tok_g, tok_s, w_s, block_expert, n_active, w_gate, w_up, w_down, l, chunk)
        x2d = _shared_and_norm(
            x1, acc, ws_gate[l].astype(BF16), ws_up[l].astype(BF16), ws_down[l].astype(BF16),
            ln2_g[l].reshape(1, d), ln2_b[l].reshape(1, d), alpha, chunk)
    return x2d.reshape(batch, seq, d)
```

```python
import functools

import jax
import jax.numpy as jnp
from jax import lax
from jax.experimental import pallas as pl
from jax.experimental.pallas import tpu as pltpu

F32 = jnp.float32
BF16 = jnp.bfloat16

D_MODEL = 1024
HEAD_DIM = 64
POOL_WINDOWS = (2, 4, 8, 16)
POOL_WIDTH = 256
POOL_GROUP = 64
DSA_WIDTH = 512
DSA_LOOKBACK = 128
DSA_DILATIONS = (1, 4, 16)
MEM_WIDTH = 256
IN_WIDTH = 2048
ROPE_THETA = 500000.0
ROT_DIM = 16
N_EXPERTS = 64
TOP_K = 8
N_GROUPS = 8
TOPK_GROUPS = 4
D_EXPERT = 256
ROUTED_SCALE = 2.5
LN_EPS = 1e-5

LANES = 128
SUBLANES = 8
ROW_VREGS = D_MODEL // LANES

NEG_BIG = -1e30

TM_PROJ = 512
QBLK = 128
DSA_UNROLL1 = 3
DSA_UNROLL16 = 4
MEM_UNROLL = 4
MOE_CHUNK = 4096
MOE_BM = 256
MOE_UNROLL = 8
MOE_EXTRA_STEPS = 3
VMEM_LIMIT = 56 * 1024 * 1024


def _cparams(sem):
    return pltpu.CompilerParams(dimension_semantics=sem, vmem_limit_bytes=VMEM_LIMIT)


def _inproj_kernel(x_ref, w_ref, c_ref, s1_ref, s2_ref, u_ref, q_ref, k_ref, v_ref, qm_ref):
    xb = x_ref[...].astype(BF16)
    h = jnp.dot(xb, w_ref[...], preferred_element_type=F32)
    c = c_ref[...]
    s1 = s1_ref[...]
    s2 = s2_ref[...]

    def rot(a):
        return a * c + pltpu.roll(a, LANES - ROT_DIM // 2, 1) * s1 + pltpu.roll(a, ROT_DIM // 2, 1) * s2

    scale = HEAD_DIM ** -0.5
    u_ref[...] = h[:, :POOL_WIDTH]
    q0 = POOL_WIDTH
    k0 = q0 + DSA_WIDTH
    v0 = k0 + DSA_WIDTH
    m0 = v0 + DSA_WIDTH
    for j in range(DSA_WIDTH // LANES):
        q_ref[:, j * LANES:(j + 1) * LANES] = rot(h[:, q0 + j * LANES:q0 + (j + 1) * LANES]) * scale
        k_ref[:, j * LANES:(j + 1) * LANES] = rot(h[:, k0 + j * LANES:k0 + (j + 1) * LANES])
    v_ref[...] = h[:, v0:m0]
    qm_ref[...] = h[:, m0:] * scale


def _in_projection(x2d, w_in_bf, rope_c, rope_s1, rope_s2, seq):
    n = x2d.shape[0]
    tm = TM_PROJ
    per_seq = seq // tm
    row = lambda i: (i, 0)
    tab = lambda i: (i % per_seq, 0)
    return pl.pallas_call(
        _inproj_kernel,
        grid=(n // tm,),
        in_specs=[
            pl.BlockSpec((tm, D_MODEL), row),
            pl.BlockSpec((D_MODEL, IN_WIDTH), lambda i: (0, 0)),
            pl.BlockSpec((tm, LANES), tab),
            pl.BlockSpec((tm, LANES), tab),
            pl.BlockSpec((tm, LANES), tab),
        ],
        out_specs=[
            pl.BlockSpec((tm, POOL_WIDTH), row),
            pl.BlockSpec((tm, DSA_WIDTH), row),
            pl.BlockSpec((tm, DSA_WIDTH), row),
            pl.BlockSpec((tm, DSA_WIDTH), row),
            pl.BlockSpec((tm, MEM_WIDTH), row),
        ],
        out_shape=[
            jax.ShapeDtypeStruct((n, POOL_WIDTH), F32),
            jax.ShapeDtypeStruct((n, DSA_WIDTH), F32),
            jax.ShapeDtypeStruct((n, DSA_WIDTH), F32),
            jax.ShapeDtypeStruct((n, DSA_WIDTH), F32),
            jax.ShapeDtypeStruct((n, MEM_WIDTH), F32),
        ],
        compiler_params=_cparams(("parallel",)),
        name="in_projection",
    )(x2d, w_in_bf, rope_c, rope_s1, rope_s2)


def _two_head_attention(q2, kk, vv, bias):
    s = lax.dot_general(q2, kk, (((1,), (1,)), ((), ())), preferred_element_type=F32)
    if bias is not None:
        s = s + bias
    m = jnp.max(s, axis=-1, keepdims=True)
    p = jnp.exp(s - m)
    l = jnp.sum(p, axis=-1, keepdims=True)
    o2 = jnp.dot(p.astype(BF16), vv, preferred_element_type=F32) / l
    lse2 = m + jnp.log(l)
    first = lax.broadcasted_iota(jnp.int32, (QBLK, LANES), 1) < HEAD_DIM
    o = jnp.where(first, o2[:QBLK], o2[QBLK:])
    lse = jnp.where(first, lse2[:QBLK], lse2[QBLK:])
    return o, lse


def _band_bias(nk, dbase):
    qi = lax.broadcasted_iota(jnp.int32, (2 * QBLK, nk), 0) & (QBLK - 1)
    kj = lax.broadcasted_iota(jnp.int32, (2 * QBLK, nk), 1)
    diff = qi - kj + dbase
    ok = lax.bitcast_convert_type(diff, jnp.uint32) <= jnp.uint32(DSA_LOOKBACK)
    return jnp.where(ok, 0.0, NEG_BIG).astype(F32)


def _dsa_kernel(q_ref, k_ref, v_ref, o_ref,
                qa1, qb1, k1, v1, qa4, qb4, k4, v4, qa16, qb16, k16, v16,
                o1, l1, o4, l4, o16, l16):
    seq = q_ref.shape[0]

    def stage(dil, qa, qb, kd, vd):
        sub = seq // dil
        for r in range(dil):
            if dil == 1:
                qv, kv, vv = q_ref[...], k_ref[...], v_ref[...]
            else:
                qv = q_ref[pl.ds(r, sub, stride=dil), :]
                kv = k_ref[pl.ds(r, sub, stride=dil), :]
                vv = v_ref[pl.ds(r, sub, stride=dil), :]
            fm = lax.broadcasted_iota(jnp.int32, (sub, LANES), 1) < HEAD_DIM
            qa[r * sub:(r + 1) * sub, :] = jnp.where(fm, qv, 0.0).astype(BF16)
            qb[r * sub:(r + 1) * sub, :] = jnp.where(fm, 0.0, qv).astype(BF16)
            kd[r * sub:(r + 1) * sub, :] = kv.astype(BF16)
            vd[r * sub:(r + 1) * sub, :] = vv.astype(BF16)

    stage(1, qa1, qb1, k1, v1)
    stage(4, qa4, qb4, k4, v4)
    stage(16, qa16, qb16, k16, v16)

    n_blocks = seq // QBLK
    causal = _band_bias(QBLK, 0)
    banded = _band_bias(2 * QBLK, QBLK)

    def block(bufs, qrow, with_prev):
        qa, qb, kd, vd, od, ld = bufs
        qrow = pl.multiple_of(qrow, QBLK)
        q2 = jnp.concatenate([qa[pl.ds(qrow, QBLK), :], qb[pl.ds(qrow, QBLK), :]], axis=0)
        if with_prev:
            krow, nk, bias = pl.multiple_of(qrow - QBLK, QBLK), 2 * QBLK, banded
        else:
            krow, nk, bias = qrow, QBLK, causal
        o, lse = _two_head_attention(q2, kd[pl.ds(krow, nk), :], vd[pl.ds(krow, nk), :], bias)
        od[pl.ds(qrow, QBLK), :] = o
        ld[pl.ds(qrow, QBLK), :] = lse

    bufs1 = (qa1, qb1, k1, v1, o1, l1)
    block(bufs1, 0, False)

    def body1(it, carry):
        for u in range(DSA_UNROLL1):
            block(bufs1, (1 + it * DSA_UNROLL1 + u) * QBLK, True)
        return carry
    lax.fori_loop(0, (n_blocks - 1) // DSA_UNROLL1, body1, 0)

    bufs4 = (qa4, qb4, k4, v4, o4, l4)
    per4 = n_blocks // 4

    def body4(r, carry):
        base = r * per4 * QBLK
        block(bufs4, base, False)
        for u in range(1, per4):
            block(bufs4, base + u * QBLK, True)
        return carry
    lax.fori_loop(0, 4, body4, 0)

    bufs16 = (qa16, qb16, k16, v16, o16, l16)

    def body16(it, carry):
        for u in range(DSA_UNROLL16):
            block(bufs16, (it * DSA_UNROLL16 + u) * QBLK, False)
        return carry
    lax.fori_loop(0, n_blocks // DSA_UNROLL16, body16, 0)

    sub16 = seq // 16
    sub4 = seq // 4
    for r in range(16):
        a_o = o1[pl.ds(r, sub16, stride=16), :]
        a_l = l1[pl.ds(r, sub16, stride=16), :]
        b_start = (r % 4) * sub4 + r // 4
        b_o = o4[pl.ds(b_start, sub16, stride=4), :]
        b_l = l4[pl.ds(b_start, sub16, stride=4), :]
        c_o = o16[r * sub16:(r + 1) * sub16, :]
        c_l = l16[r * sub16:(r + 1) * sub16, :]
        m = jnp.maximum(jnp.maximum(a_l, b_l), c_l)
        ea = jnp.exp(a_l - m)
        eb = jnp.exp(b_l - m)
        ec = jnp.exp(c_l - m)
        out = (ea * a_o + eb * b_o + ec * c_o) / (ea + eb + ec)
        o_ref[pl.ds(r, sub16, stride=16), :] = out


def _dilated_attention(q, k, v, batch, seq):
    n = q.shape[0]
    pairs = DSA_WIDTH // LANES
    blk = pl.BlockSpec((seq, LANES), lambda b, j: (b, j))
    bf = lambda: pltpu.VMEM((seq, LANES), BF16)
    ff = lambda: pltpu.VMEM((seq, LANES), F32)
    return pl.pallas_call(
        _dsa_kernel,
        grid=(batch, pairs),
        in_specs=[blk, blk, blk],
        out_specs=blk,
        out_shape=jax.ShapeDtypeStruct((n, DSA_WIDTH), F32),
        scratch_shapes=[bf() for _ in range(12)] + [ff() for _ in range(6)],
        compiler_params=_cparams(("parallel", "parallel")),
        name="dilated_attention",
    )(q, k, v)


def _poolmem_kernel(u_ref, qm_ref, mem_ref, wkv_ref, wpool_ref, pscale_ref, y_ref, kv_s, qa_s, qb_s):
    seq = u_ref.shape[0]
    u = u_ref[...]
    rows = lax.broadcasted_iota(jnp.int32, (seq, POOL_WIDTH), 0)
    grp = lax.broadcasted_iota(jnp.int32, (seq, POOL_WIDTH), 1) // POOL_GROUP

    def shifted(a, kk):
        return jnp.where(rows >= kk, pltpu.roll(a, kk, 0), 0.0)

    s2 = u + shifted(u, 1)
    s4 = s2 + shifted(s2, 2)
    s8 = s4 + shifted(s4, 4)
    s16 = s8 + shifted(s8, 8)
    wsum = jnp.where(grp == 0, s2, jnp.where(grp == 1, s4, jnp.where(grp == 2, s8, s16)))
    win = jnp.where(grp == 0, 2, jnp.where(grp == 1, 4, jnp.where(grp == 2, 8, 16)))
    cnt = jnp.minimum(rows + 1, win).astype(F32)
    pooled = wsum / cnt - u
    mixed = jnp.dot(pooled.astype(BF16), wpool_ref[...], preferred_element_type=F32) * pscale_ref[...]
    y_ref[:, :POOL_WIDTH] = mixed

    kv_s[...] = jnp.dot(mem_ref[...].astype(BF16), wkv_ref[...], preferred_element_type=F32).astype(BF16)
    first = lax.broadcasted_iota(jnp.int32, (seq, LANES), 1) < HEAD_DIM
    for j in range(MEM_WIDTH // LANES):
        qv = qm_ref[:, j * LANES:(j + 1) * LANES]
        qa_s[...] = jnp.where(first, qv, 0.0).astype(BF16)
        qb_s[...] = jnp.where(first, 0.0, qv).astype(BF16)
        kk = kv_s[:, j * LANES:(j + 1) * LANES]
        vv = kv_s[:, MEM_WIDTH + j * LANES:MEM_WIDTH + (j + 1) * LANES]

        def body(it, carry):
            for u in range(MEM_UNROLL):
                qrow = pl.multiple_of((it * MEM_UNROLL + u) * QBLK, QBLK)
                q2 = jnp.concatenate([qa_s[pl.ds(qrow, QBLK), :], qb_s[pl.ds(qrow, QBLK), :]], axis=0)
                o, _ = _two_head_attention(q2, kk, vv, None)
                y_ref[pl.ds(qrow, QBLK), pl.ds(POOL_WIDTH + j * LANES, LANES)] = o
            return carry
        lax.fori_loop(0, seq // (QBLK * MEM_UNROLL), body, 0)


def _pool_and_memory(u, qm, mem2d, wkv_bf, wpool_bd, pscale, batch, seq):
    n = u.shape[0]
    mem_len = mem2d.shape[0] // batch
    row = lambda b: (b, 0)
    fix = lambda b: (0, 0)
    return pl.pallas_call(
        _poolmem_kernel,
        grid=(batch,),
        in_specs=[
            pl.BlockSpec((seq, POOL_WIDTH), row),
            pl.BlockSpec((seq, MEM_WIDTH), row),
            pl.BlockSpec((mem_len, D_MODEL), row),
            pl.BlockSpec((D_MODEL, 2 * MEM_WIDTH), fix),
            pl.BlockSpec((POOL_WIDTH, POOL_WIDTH), fix),
            pl.BlockSpec((1, POOL_WIDTH), fix),
        ],
        out_specs=pl.BlockSpec((seq, POOL_WIDTH + MEM_WIDTH), row),
        out_shape=jax.ShapeDtypeStruct((n, POOL_WIDTH + MEM_WIDTH), F32),
        scratch_shapes=[
            pltpu.VMEM((mem_len, 2 * MEM_WIDTH), BF16),
            pltpu.VMEM((seq, LANES), BF16),
            pltpu.VMEM((seq, LANES), BF16),
        ],
        compiler_params=_cparams(("parallel",)),
        name="pool_memory",
    )(u, qm, mem2d, wkv_bf, wpool_bd, pscale)


def _layer_norm(z, g, b):
    mu = jnp.mean(z, axis=-1, keepdims=True)
    zc = z - mu
    var = jnp.mean(zc * zc, axis=-1, keepdims=True)
    return zc * lax.rsqrt(var + LN_EPS) * g + b


def _store_token_rows(dst_ref, val):
    tm = val.shape[0]
    for s in range(ROW_VREGS):
        dst_ref[pl.ds(s, tm, stride=ROW_VREGS), :] = val[:, s * LANES:(s + 1) * LANES]


def _load_token_rows(src_ref, tm):
    return jnp.concatenate(
        [src_ref[pl.ds(s, tm, stride=ROW_VREGS), :] for s in range(ROW_VREGS)], axis=1)


def _route(x1, wrh, wrl, rbias, ridx_ref, rw_ref):
    tm = x1.shape[0]
    xh = x1.astype(BF16)
    xl = (x1 - xh.astype(F32)).astype(BF16)
    nt = (((1,), (1,)), ((), ()))
    logits = (lax.dot_general(wrh, xh, nt, preferred_element_type=F32)
              + lax.dot_general(wrh, xl, nt, preferred_element_type=F32)
              + lax.dot_general(wrl, xh, nt, preferred_element_type=F32))
    scores = jax.nn.sigmoid(logits)
    biased = scores + rbias

    per = N_EXPERTS // N_GROUPS
    io8 = lax.broadcasted_iota(jnp.int32, (per, tm), 0)
    gscores = []
    for g in range(N_GROUPS):
        vg = biased[g * per:(g + 1) * per]
        m1 = jnp.max(vg, axis=0, keepdims=True)
        i1 = jnp.min(jnp.where(vg == m1, io8, per), axis=0, keepdims=True)
        m2 = jnp.max(jnp.where(io8 == i1, -jnp.inf, vg), axis=0, keepdims=True)
        gscores.append(m1 + m2)
    gs = jnp.concatenate(gscores, axis=0)
    iog = lax.broadcasted_iota(jnp.int32, (N_GROUPS, tm), 0)
    grank = jnp.zeros((N_GROUPS, tm), F32)
    for g in range(N_GROUPS):
        sg = gs[g:g + 1]
        tie = jnp.where(iog > g, 1.0, 0.0)
        grank = grank + jnp.where(sg > gs, 1.0, jnp.where(sg == gs, tie, 0.0))
    gsel = grank < TOPK_GROUPS
    masked = jnp.concatenate(
        [jnp.where(gsel[g:g + 1], biased[g * per:(g + 1) * per], -jnp.inf) for g in range(N_GROUPS)], axis=0)

    ioe = lax.broadcasted_iota(jnp.int32, (N_EXPERTS, tm), 0)
    rank = jnp.zeros((N_EXPERTS, tm), F32)
    for e in range(N_EXPERTS):
        me = masked[e:e + 1]
        tie = jnp.where(ioe > e, 1.0, 0.0)
        rank = rank + jnp.where(me > masked, 1.0, jnp.where(me == masked, tie, 0.0))

    idx_rows, w_rows = [], []
    for kk in range(TOP_K):
        hit = rank == float(kk)
        idx_rows.append(jnp.sum(jnp.where(hit, ioe, 0), axis=0, keepdims=True))
        w_rows.append(jnp.sum(jnp.where(hit, scores, 0.0), axis=0, keepdims=True))
    top_w = jnp.concatenate(w_rows, axis=0)
    denom = jnp.sum(top_w, axis=0, keepdims=True)
    ridx_ref[...] = jnp.concatenate(idx_rows, axis=0)
    rw_ref[...] = top_w / denom * ROUTED_SCALE


def _outproj_kernel(alpha, x_ref, ypm_ref, ydsa_ref, wo_ref, g_ref, b_ref, wrh_ref, wrl_ref, rb_ref,
                    x1_ref, x1g_ref, ridx_ref, rw_ref):
    y = jnp.concatenate([ypm_ref[:, :POOL_WIDTH], ydsa_ref[...], ypm_ref[:, POOL_WIDTH:]], axis=1)
    mix = jnp.dot(y.astype(BF16), wo_ref[...], preferred_element_type=F32)
    x1 = _layer_norm(alpha * x_ref[...] + mix, g_ref[...], b_ref[...])
    x1_ref[...] = x1
    _store_token_rows(x1g_ref, x1)
    _route(x1, wrh_ref[...], wrl_ref[...], rb_ref[...], ridx_ref, rw_ref)


def _out_projection(x2d, ypm, ydsa, wo_bf, g, b, wrh, wrl, rbias, alpha):
    n = x2d.shape[0]
    tm = TM_PROJ
    row = lambda i: (i, 0)
    fix = lambda i: (0, 0)
    col = lambda i: (0, i)
    return pl.pallas_call(
        functools.partial(_outproj_kernel, alpha),
        grid=(n // tm,),
        in_specs=[
            pl.BlockSpec((tm, D_MODEL), row),
            pl.BlockSpec((tm, POOL_WIDTH + MEM_WIDTH), row),
            pl.BlockSpec((tm, DSA_WIDTH), row),
            pl.BlockSpec((D_MODEL, D_MODEL), fix),
            pl.BlockSpec((1, D_MODEL), fix),
            pl.BlockSpec((1, D_MODEL), fix),
            pl.BlockSpec((N_EXPERTS, D_MODEL), fix),
            pl.BlockSpec((N_EXPERTS, D_MODEL), fix),
            pl.BlockSpec((N_EXPERTS, 1), fix),
        ],
        out_specs=[
            pl.BlockSpec((tm, D_MODEL), row),
            pl.BlockSpec((tm * ROW_VREGS, LANES), row),
            pl.BlockSpec((TOP_K, tm), col),
            pl.BlockSpec((TOP_K, tm), col),
        ],
        out_shape=[
            jax.ShapeDtypeStruct((n, D_MODEL), F32),
            jax.ShapeDtypeStruct((n * ROW_VREGS, LANES), F32),
            jax.ShapeDtypeStruct((TOP_K, n), jnp.int32),
            jax.ShapeDtypeStruct((TOP_K, n), F32),
        ],
        compiler_params=_cparams(("parallel",)),
        name="out_projection_router",
    )(x2d, ypm, ydsa, wo_bf, g, b, wrh, wrl, rbias)


def _moe_blocks(chunk):
    return chunk * TOP_K // MOE_BM + N_EXPERTS


def _as_column(w_row):
    bm = w_row.shape[1]
    eye = lax.broadcasted_iota(jnp.int32, (bm, bm), 0) == lax.broadcasted_iota(jnp.int32, (bm, bm), 1)
    return jnp.sum(jnp.where(eye, w_row, 0.0), axis=1, keepdims=True)


def _moe_kernel(eb_ref, nact_ref, tok_ref, wrow_ref, xg_ref, wg_ref, wu_ref, wd_ref, acc_ref,
                rows0, rows1, y0, y1):
    c = pl.program_id(0)
    j = pl.program_id(1)

    @pl.when(j == 0)
    def _():
        acc_ref[...] = jnp.zeros(acc_ref.shape, F32)

    @pl.when(jnp.logical_and(c == 0, j == 0))
    def _():
        for buf in (rows0, rows1, y0, y1):
            buf[...] = jnp.zeros(buf.shape, F32)

    def step(rows_w, rows_r, y_w, y_r):
        for i in range(MOE_BM):
            src = pl.multiple_of(tok_ref[0, 0, i], ROW_VREGS)
            rows_w[i * ROW_VREGS:(i + 1) * ROW_VREGS, :] = xg_ref[pl.ds(src, ROW_VREGS), :]

        for g0 in range(0, MOE_BM, MOE_UNROLL):
            upd = []
            for i in range(g0, g0 + MOE_UNROLL):
                dst = pl.multiple_of(tok_ref[0, 0, MOE_BM + i], ROW_VREGS)
                upd.append((dst, acc_ref[pl.ds(dst, ROW_VREGS), :] + y_r[i * ROW_VREGS:(i + 1) * ROW_VREGS, :]))
            for dst, val in reversed(upd):
                acc_ref[pl.ds(dst, ROW_VREGS), :] = val

        xb = _load_token_rows(rows_r, MOE_BM).astype(BF16)
        gg = jnp.dot(xb, wg_ref[...], preferred_element_type=F32)
        uu = jnp.dot(xb, wu_ref[...], preferred_element_type=F32)
        hh = gg * jax.nn.sigmoid(gg) * uu * _as_column(wrow_ref[0])
        yy = jnp.dot(hh.astype(BF16), wd_ref[...], preferred_element_type=F32)
        _store_token_rows(y_w, yy)

    active = j < nact_ref[c] + 3

    @pl.when(jnp.logical_and(active, j % 2 == 0))
    def _():
        step(rows0, rows1, y1, y0)

    @pl.when(jnp.logical_and(active, j % 2 == 1))
    def _():
        step(rows1, rows0, y0, y1)


def _routed_experts(x1g, step_tok, step_w, block_expert, n_active, w_gate, w_up, w_down, layer, chunk):
    n_tok = x1g.shape[0] // ROW_VREGS
    n_chunks = n_tok // chunk
    nb = _moe_blocks(chunk)
    steps = nb + MOE_EXTRA_STEPS
    acc_rows = chunk * ROW_VREGS

    def per_step(c, j, eb, na):
        return (c * steps + j, 0, 0)

    def expert_w(c, j, eb, na):
        return (layer, eb[c * nb + jnp.clip(j - 1, 0, nb - 1)], 0, 0)

    buf = lambda: pltpu.VMEM((MOE_BM * ROW_VREGS, LANES), F32)
    grid_spec = pltpu.PrefetchScalarGridSpec(
        num_scalar_prefetch=2,
        grid=(n_chunks, steps),
        in_specs=[
            pl.BlockSpec((1, 1, 2 * MOE_BM), per_step, memory_space=pltpu.SMEM),
            pl.BlockSpec((1, 1, MOE_BM), per_step),
            pl.BlockSpec((chunk * ROW_VREGS, LANES), lambda c, j, eb, na: (c, 0), pipeline_mode=pl.Buffered(1)),
            pl.BlockSpec((None, None, D_MODEL, D_EXPERT), expert_w),
            pl.BlockSpec((None, None, D_MODEL, D_EXPERT), expert_w),
            pl.BlockSpec((None, None, D_EXPERT, D_MODEL), expert_w),
        ],
        out_specs=pl.BlockSpec((None, acc_rows, LANES), lambda c, j, eb, na: (c, 0, 0),
                               pipeline_mode=pl.Buffered(1)),
        scratch_shapes=[buf(), buf(), buf(), buf()],
    )
    return pl.pallas_call(
        _moe_kernel,
        grid_spec=grid_spec,
        out_shape=jax.ShapeDtypeStruct((n_chunks, acc_rows, LANES), F32),
        compiler_params=_cparams(("arbitrary", "arbitrary")),
        name="routed_experts",
    )(block_expert, n_active, step_tok, step_w, x1g, w_gate, w_up, w_down)


def _shared_kernel(alpha, x_ref, acc_ref, wg_ref, wu_ref, wd_ref, g_ref, b_ref, o_ref):
    x1 = x_ref[...]
    tm = x1.shape[0]
    xb = x1.astype(BF16)
    gg = jnp.dot(xb, wg_ref[...], preferred_element_type=F32)
    uu = jnp.dot(xb, wu_ref[...], preferred_element_type=F32)
    hh = (gg * jax.nn.sigmoid(gg) * uu).astype(BF16)
    shared = jnp.dot(hh, wd_ref[...], preferred_element_type=F32)
    routed = _load_token_rows(acc_ref, tm)
    o_ref[...] = _layer_norm(alpha * x1 + (routed + shared), g_ref[...], b_ref[...])


def _shared_and_norm(x1, acc, wsg, wsu, wsd, g, b, alpha, chunk):
    n = x1.shape[0]
    tm = TM_PROJ
    per_chunk = chunk // tm
    row = lambda i: (i, 0)
    fix = lambda i: (0, 0)
    return pl.pallas_call(
        functools.partial(_shared_kernel, alpha),
        grid=(n // tm,),
        in_specs=[
            pl.BlockSpec((tm, D_MODEL), row),
            pl.BlockSpec((None, tm * ROW_VREGS, LANES), lambda i: (i // per_chunk, i % per_chunk, 0)),
            pl.BlockSpec((D_MODEL, D_EXPERT), fix),
            pl.BlockSpec((D_MODEL, D_EXPERT), fix),
            pl.BlockSpec((D_EXPERT, D_MODEL), fix),
            pl.BlockSpec((1, D_MODEL), fix),
            pl.BlockSpec((1, D_MODEL), fix),
        ],
        out_specs=pl.BlockSpec((tm, D_MODEL), row),
        out_shape=jax.ShapeDtypeStruct((n, D_MODEL), F32),
        compiler_params=_cparams(("parallel",)),
        name="shared_expert_norm",
    )(x1, acc, wsg, wsu, wsd, g, b)


def _rope_tables(seq):
    half = ROT_DIM // 2
    pos = jnp.arange(seq, dtype=F32)
    inv = ROPE_THETA ** (-jnp.arange(0, ROT_DIM, 2, dtype=F32) / ROT_DIM)
    ang = pos[:, None] * inv[None, :]
    cos, sin = jnp.cos(ang), jnp.sin(ang)
    pad = HEAD_DIM - ROT_DIM
    ones = jnp.ones((seq, pad), F32)
    zeros = jnp.zeros((seq, pad), F32)
    zh = jnp.zeros((seq, half), F32)
    c_head = jnp.concatenate([cos, cos, ones], axis=1)
    s1_head = jnp.concatenate([-sin, zh, zeros], axis=1)
    s2_head = jnp.concatenate([zh, sin, zeros], axis=1)
    rep = LANES // HEAD_DIM
    return (jnp.tile(c_head, (1, rep)), jnp.tile(s1_head, (1, rep)), jnp.tile(s2_head, (1, rep)))


def _dispatch_lists(ridx, rw, chunk):
    n = ridx.shape[1]
    n_chunks = n // chunk
    e_flat = ridx.T.reshape(n_chunks, chunk * TOP_K)
    w_flat = rw.T.reshape(n_chunks, chunk * TOP_K)
    n_asg = chunk * TOP_K
    nb = _moe_blocks(chunk)
    pos = lax.broadcasted_iota(jnp.int32, e_flat.shape, 1)
    _, order = lax.sort((e_flat, pos), dimension=1, is_stable=True, num_keys=1)
    experts = jnp.arange(N_EXPERTS, dtype=jnp.int32)
    counts = jnp.sum((e_flat[:, :, None] == experts).astype(jnp.int32), axis=1)
    start = jnp.cumsum(counts, axis=1) - counts
    n_blk = (counts + MOE_BM - 1) // MOE_BM
    b_end = jnp.cumsum(n_blk, axis=1)
    b_start = b_end - n_blk
    n_active = b_end[:, -1]

    jb = jnp.arange(nb, dtype=jnp.int32)
    eb = jnp.minimum(jnp.sum((jb[None, :, None] >= b_end[:, None, :]).astype(jnp.int32), axis=2), N_EXPERTS - 1)
    take = lambda a: jnp.take_along_axis(a, eb, axis=1)
    row = (jb[None, :] - take(b_start))[:, :, None] * MOE_BM + jnp.arange(MOE_BM, dtype=jnp.int32)
    valid = (jb[None, :, None] < n_active[:, None, None]) & (row < take(counts)[:, :, None])
    src = jnp.clip(take(start)[:, :, None] + row, 0, n_asg - 1).reshape(n_chunks, -1)
    asg = jnp.take_along_axis(order, src, axis=1)
    tok = jnp.where(valid, (asg // TOP_K).reshape(valid.shape), 0) * ROW_VREGS
    wgt = jnp.where(valid, jnp.take_along_axis(w_flat, asg, axis=1).reshape(valid.shape), 0.0)

    def shifted(a, lag):
        pad = lambda k: jnp.zeros((n_chunks, k, MOE_BM), a.dtype)
        return jnp.concatenate([pad(lag), a, pad(MOE_EXTRA_STEPS - lag)], axis=1)
    steps = nb + MOE_EXTRA_STEPS
    step_tok = jnp.concatenate([shifted(tok, 0), shifted(tok, 2)], axis=2).reshape(n_chunks * steps, 1, 2 * MOE_BM)
    step_w = shifted(wgt, 1).reshape(n_chunks * steps, 1, MOE_BM)
    return step_tok, step_w, eb.reshape(-1), n_active


def kernel(x, mem, w_in, w_pool, pool_scale, w_mem_kv, w_out, ln1_g, ln1_b, w_router, router_bias,
           w_gate, w_up, w_down, ws_gate, ws_up, ws_down, ln2_g, ln2_b):
    batch, seq, d = x.shape
    depth = w_in.shape[0]
    assert d == D_MODEL and seq % (16 * QBLK) == 0 and seq % TM_PROJ == 0
    assert (seq // QBLK - 1) % DSA_UNROLL1 == 0 and (seq // QBLK) % max(DSA_UNROLL16, MEM_UNROLL) == 0
    n = batch * seq
    chunk = min(MOE_CHUNK, n)
    assert n % chunk == 0 and chunk % TM_PROJ == 0
    alpha = float((2 * depth) ** 0.25)

    rope_c, rope_s1, rope_s2 = _rope_tables(seq)
    wg_bf, wu_bf, wd_bf = w_gate.astype(BF16), w_up.astype(BF16), w_down.astype(BF16)
    x2d = x.reshape(n, d)
    mem2d = mem.reshape(batch * mem.shape[1], d)

    for l in range(depth):
        w_in_bf = w_in[l].astype(BF16)
        wkv_bf = w_mem_kv[l].astype(BF16)
        wo_bf = w_out[l].astype(BF16)
        wpool_bd = jax.scipy.linalg.block_diag(*[w_pool[l, g] for g in range(len(POOL_WINDOWS))]).astype(BF16)
        pscale = pool_scale[l].reshape(1, POOL_WIDTH)
        wr_t = w_router[l].T
        wrh = wr_t.astype(BF16)
        wrl = (wr_t - wrh.astype(F32)).astype(BF16)
        rbias = router_bias[l].reshape(N_EXPERTS, 1)

        u, q, k, v, qm = _in_projection(x2d, w_in_bf, rope_c, rope_s1, rope_s2, seq)
        ydsa = _dilated_attention(q, k, v, batch, seq)
        ypm = _pool_and_memory(u, qm, mem2d, wkv_bf, wpool_bd, pscale, batch, seq)
        x1, x1g, ridx, rw = _out_projection(
            x2d, ypm, ydsa, wo_bf, ln1_g[l].reshape(1, d), ln1_b[l].reshape(1, d), wrh, wrl, rbias, alpha)
        step_tok, step_w, block_expert, n_active = _dispatch_lists(ridx, rw, chunk)
        acc = _routed_experts(x1g, step_tok, step_w, block_expert, n_active, wg_bf, wu_bf, wd_bf, l, chunk)
        x2d = _shared_and_norm(
            x1, acc, ws_gate[l].astype(BF16), ws_up[l].astype(BF16), ws_down[l].astype(BF16),
            ln2_g[l].reshape(1, d), ln2_b[l].reshape(1, d), alpha, chunk)
    return x2d.reshape(batch, seq, d)
```

```python
import functools

import jax
import jax.numpy as jnp
from jax import lax
from jax.experimental import pallas as pl
from jax.experimental.pallas import tpu as pltpu

F32 = jnp.float32
BF16 = jnp.bfloat16

D_MODEL = 1024
HEAD_DIM = 64
POOL_WINDOWS = (2, 4, 8, 16)
POOL_WIDTH = 256
POOL_GROUP = 64
DSA_WIDTH = 512
DSA_LOOKBACK = 128
DSA_DILATIONS = (1, 4, 16)
MEM_WIDTH = 256
IN_WIDTH = 2048
ROPE_THETA = 500000.0
ROT_DIM = 16
N_EXPERTS = 64
TOP_K = 8
N_GROUPS = 8
TOPK_GROUPS = 4
D_EXPERT = 256
ROUTED_SCALE = 2.5
LN_EPS = 1e-5

LANES = 128
SUBLANES = 8
SMEM_BLOCK_1D = 1024
ROW_VREGS = D_MODEL // LANES

NEG_BIG = -1e30

TM_PROJ = 512
QBLK = 128
DSA_UNROLL1 = 15
DSA_RES4 = 4
DSA_UNROLL16 = 16
MEM_UNROLL = 16
MOE_CHUNK = 4096
MOE_BM = 256
MOE_UNROLL = 8
MOE_LAG_EXPERT = 1
MOE_LAG_SCATTER = 2
MOE_EXTRA_STEPS = MOE_LAG_SCATTER + 1
VMEM_LIMIT = 56 * 1024 * 1024


def _cparams(sem):
    return pltpu.CompilerParams(dimension_semantics=sem, vmem_limit_bytes=VMEM_LIMIT)


def _inproj_kernel(x_ref, w_ref, c_ref, s1_ref, s2_ref, u_ref, q_ref, k_ref, v_ref, qm_ref):
    xb = x_ref[...].astype(BF16)
    h = jnp.dot(xb, w_ref[...], preferred_element_type=F32)
    c = c_ref[...]
    s1 = s1_ref[...]
    s2 = s2_ref[...]

    def rot(a):
        return a * c + pltpu.roll(a, LANES - ROT_DIM // 2, 1) * s1 + pltpu.roll(a, ROT_DIM // 2, 1) * s2

    scale = HEAD_DIM ** -0.5
    u_ref[...] = h[:, :POOL_WIDTH]
    q0 = POOL_WIDTH
    k0 = q0 + DSA_WIDTH
    v0 = k0 + DSA_WIDTH
    m0 = v0 + DSA_WIDTH
    for j in range(DSA_WIDTH // LANES):
        q_ref[:, j * LANES:(j + 1) * LANES] = rot(h[:, q0 + j * LANES:q0 + (j + 1) * LANES]) * scale
        k_ref[:, j * LANES:(j + 1) * LANES] = rot(h[:, k0 + j * LANES:k0 + (j + 1) * LANES])
    v_ref[...] = h[:, v0:m0]
    qm_ref[...] = h[:, m0:] * scale


def _in_projection(x2d, w_in_bf, rope_c, rope_s1, rope_s2, seq):
    n = x2d.shape[0]
    tm = TM_PROJ
    per_seq = seq // tm
    row = lambda i: (i, 0)
    tab = lambda i: (i % per_seq, 0)
    return pl.pallas_call(
        _inproj_kernel,
        grid=(n // tm,),
        in_specs=[
            pl.BlockSpec((tm, D_MODEL), row),
            pl.BlockSpec((D_MODEL, IN_WIDTH), lambda i: (0, 0)),
            pl.BlockSpec((tm, LANES), tab),
            pl.BlockSpec((tm, LANES), tab),
            pl.BlockSpec((tm, LANES), tab),
        ],
        out_specs=[
            pl.BlockSpec((tm, POOL_WIDTH), row),
            pl.BlockSpec((tm, DSA_WIDTH), row),
            pl.BlockSpec((tm, DSA_WIDTH), row),
            pl.BlockSpec((tm, DSA_WIDTH), row),
            pl.BlockSpec((tm, MEM_WIDTH), row),
        ],
        out_shape=[
            jax.ShapeDtypeStruct((n, POOL_WIDTH), F32),
            jax.ShapeDtypeStruct((n, DSA_WIDTH), F32),
            jax.ShapeDtypeStruct((n, DSA_WIDTH), F32),
            jax.ShapeDtypeStruct((n, DSA_WIDTH), F32),
            jax.ShapeDtypeStruct((n, MEM_WIDTH), F32),
        ],
        compiler_params=_cparams(("parallel",)),
        name="in_projection",
    )(x2d, w_in_bf, rope_c, rope_s1, rope_s2)


def _two_head_attention(q2, kk, vv, bias):
    s = lax.dot_general(q2, kk, (((1,), (1,)), ((), ())), preferred_element_type=F32)
    if bias is not None:
        s = s + bias
    m = jnp.max(s, axis=-1, keepdims=True)
    p = jnp.exp(s - m)
    l = jnp.sum(p, axis=-1, keepdims=True)
    o2 = jnp.dot(p.astype(BF16), vv, preferred_element_type=F32) / l
    lse2 = m + jnp.log(l)
    first = lax.broadcasted_iota(jnp.int32, (QBLK, LANES), 1) < HEAD_DIM
    o = jnp.where(first, o2[:QBLK], o2[QBLK:])
    lse = jnp.where(first, lse2[:QBLK], lse2[QBLK:])
    return o, lse


def _band_bias(nk, dbase):
    qi = lax.broadcasted_iota(jnp.int32, (2 * QBLK, nk), 0) & (QBLK - 1)
    kj = lax.broadcasted_iota(jnp.int32, (2 * QBLK, nk), 1)
    diff = qi - kj + dbase
    ok = lax.bitcast_convert_type(diff, jnp.uint32) <= jnp.uint32(DSA_LOOKBACK)
    return jnp.where(ok, 0.0, NEG_BIG).astype(F32)


def _dsa_kernel(q_ref, k_ref, v_ref, o_ref,
                qa1, qb1, k1, v1, qa4, qb4, k4, v4, qa16, qb16, k16, v16,
                o1, l1, o4, l4, o16, l16):
    seq = q_ref.shape[0]

    def stage(dil, qa, qb, kd, vd):
        sub = seq // dil
        for r in range(dil):
            if dil == 1:
                qv, kv, vv = q_ref[...], k_ref[...], v_ref[...]
            else:
                qv = q_ref[pl.ds(r, sub, stride=dil), :]
                kv = k_ref[pl.ds(r, sub, stride=dil), :]
                vv = v_ref[pl.ds(r, sub, stride=dil), :]
            fm = lax.broadcasted_iota(jnp.int32, (sub, LANES), 1) < HEAD_DIM
            qa[r * sub:(r + 1) * sub, :] = jnp.where(fm, qv, 0.0).astype(BF16)
            qb[r * sub:(r + 1) * sub, :] = jnp.where(fm, 0.0, qv).astype(BF16)
            kd[r * sub:(r + 1) * sub, :] = kv.astype(BF16)
            vd[r * sub:(r + 1) * sub, :] = vv.astype(BF16)

    stage(1, qa1, qb1, k1, v1)
    stage(4, qa4, qb4, k4, v4)
    stage(16, qa16, qb16, k16, v16)

    n_blocks = seq // QBLK
    causal = _band_bias(QBLK, 0)
    banded = _band_bias(2 * QBLK, QBLK)

    def block(bufs, qrow, with_prev):
        qa, qb, kd, vd, od, ld = bufs
        qrow = pl.multiple_of(qrow, QBLK)
        q2 = jnp.concatenate([qa[pl.ds(qrow, QBLK), :], qb[pl.ds(qrow, QBLK), :]], axis=0)
        if with_prev:
            krow, nk, bias = pl.multiple_of(qrow - QBLK, QBLK), 2 * QBLK, banded
        else:
            krow, nk, bias = qrow, QBLK, causal
        o, lse = _two_head_attention(q2, kd[pl.ds(krow, nk), :], vd[pl.ds(krow, nk), :], bias)
        od[pl.ds(qrow, QBLK), :] = o
        ld[pl.ds(qrow, QBLK), :] = lse

    bufs1 = (qa1, qb1, k1, v1, o1, l1)
    block(bufs1, 0, False)

    def body1(it, carry):
        for u in range(DSA_UNROLL1):
            block(bufs1, (1 + it * DSA_UNROLL1 + u) * QBLK, True)
        return carry
    lax.fori_loop(0, (n_blocks - 1) // DSA_UNROLL1, body1, 0)

    bufs4 = (qa4, qb4, k4, v4, o4, l4)
    per4 = n_blocks // 4

    def body4(it, carry):
        for rr in range(DSA_RES4):
            base = (it * DSA_RES4 + rr) * per4 * QBLK
            block(bufs4, base, False)
            for u in range(1, per4):
                block(bufs4, base + u * QBLK, True)
        return carry
    lax.fori_loop(0, 4 // DSA_RES4, body4, 0)

    bufs16 = (qa16, qb16, k16, v16, o16, l16)

    def body16(it, carry):
        for u in range(DSA_UNROLL16):
            block(bufs16, (it * DSA_UNROLL16 + u) * QBLK, False)
        return carry
    lax.fori_loop(0, n_blocks // DSA_UNROLL16, body16, 0)

    sub16 = seq // 16
    sub4 = seq // 4
    for r in range(16):
        nat = pl.ds(r, sub16, stride=16)
        res4 = pl.ds((r % 4) * sub4 + r // 4, sub16, stride=4)
        res16 = pl.ds(r * sub16, sub16)
        a_l, b_l, c_l = l1[nat, :], l4[res4, :], l16[res16, :]
        m = jnp.maximum(jnp.maximum(a_l, b_l), c_l)
        ea = jnp.exp(a_l - m)
        eb = jnp.exp(b_l - m)
        ec = jnp.exp(c_l - m)
        out = (ea * o1[nat, :] + eb * o4[res4, :] + ec * o16[res16, :]) / (ea + eb + ec)
        o_ref[nat, :] = out


def _dilated_attention(q, k, v, batch, seq):
    n = q.shape[0]
    pairs = DSA_WIDTH // LANES
    blk = pl.BlockSpec((seq, LANES), lambda b, j: (b, j))
    bf = lambda: pltpu.VMEM((seq, LANES), BF16)
    ff = lambda: pltpu.VMEM((seq, LANES), F32)
    return pl.pallas_call(
        _dsa_kernel,
        grid=(batch, pairs),
        in_specs=[blk, blk, blk],
        out_specs=blk,
        out_shape=jax.ShapeDtypeStruct((n, DSA_WIDTH), F32),
        scratch_shapes=[bf() for _ in range(12)] + [ff() for _ in range(6)],
        compiler_params=_cparams(("parallel", "parallel")),
        name="dilated_attention",
    )(q, k, v)


def _poolmem_kernel(u_ref, qm_ref, mem_ref, wkv_ref, wpool_ref, pscale_ref, y_ref, kv_s, qa_s, qb_s):
    seq = u_ref.shape[0]
    u = u_ref[...]
    rows = lax.broadcasted_iota(jnp.int32, (seq, POOL_WIDTH), 0)
    grp = lax.broadcasted_iota(jnp.int32, (seq, POOL_WIDTH), 1) // POOL_GROUP

    def shifted(a, kk):
        return jnp.where(rows >= kk, pltpu.roll(a, kk, 0), 0.0)

    s2 = u + shifted(u, 1)
    s4 = s2 + shifted(s2, 2)
    s8 = s4 + shifted(s4, 4)
    s16 = s8 + shifted(s8, 8)
    wsum = jnp.where(grp == 0, s2, jnp.where(grp == 1, s4, jnp.where(grp == 2, s8, s16)))
    win = jnp.where(grp == 0, 2, jnp.where(grp == 1, 4, jnp.where(grp == 2, 8, 16)))
    cnt = jnp.minimum(rows + 1, win).astype(F32)
    pooled = wsum / cnt - u
    mixed = jnp.dot(pooled.astype(BF16), wpool_ref[...], preferred_element_type=F32) * pscale_ref[...]
    y_ref[:, :POOL_WIDTH] = mixed

    kv_s[...] = jnp.dot(mem_ref[...].astype(BF16), wkv_ref[...], preferred_element_type=F32).astype(BF16)
    first = lax.broadcasted_iota(jnp.int32, (seq, LANES), 1) < HEAD_DIM
    for j in range(MEM_WIDTH // LANES):
        qv = qm_ref[:, j * LANES:(j + 1) * LANES]
        qa_s[...] = jnp.where(first, qv, 0.0).astype(BF16)
        qb_s[...] = jnp.where(first, 0.0, qv).astype(BF16)
        kk = kv_s[:, j * LANES:(j + 1) * LANES]
        vv = kv_s[:, MEM_WIDTH + j * LANES:MEM_WIDTH + (j + 1) * LANES]

        def body(it, carry):
            for u in range(MEM_UNROLL):
                qrow = pl.multiple_of((it * MEM_UNROLL + u) * QBLK, QBLK)
                q2 = jnp.concatenate([qa_s[pl.ds(qrow, QBLK), :], qb_s[pl.ds(qrow, QBLK), :]], axis=0)
                o, _ = _two_head_attention(q2, kk, vv, None)
                y_ref[pl.ds(qrow, QBLK), pl.ds(POOL_WIDTH + j * LANES, LANES)] = o
            return carry
        lax.fori_loop(0, seq // (QBLK * MEM_UNROLL), body, 0)


def _pool_and_memory(u, qm, mem2d, wkv_bf, wpool_bd, pscale, batch, seq):
    n = u.shape[0]
    mem_len = mem2d.shape[0] // batch
    row = lambda b: (b, 0)
    fix = lambda b: (0, 0)
    return pl.pallas_call(
        _poolmem_kernel,
        grid=(batch,),
        in_specs=[
            pl.BlockSpec((seq, POOL_WIDTH), row),
            pl.BlockSpec((seq, MEM_WIDTH), row),
            pl.BlockSpec((mem_len, D_MODEL), row),
            pl.BlockSpec((D_MODEL, 2 * MEM_WIDTH), fix),
            pl.BlockSpec((POOL_WIDTH, POOL_WIDTH), fix),
            pl.BlockSpec((1, POOL_WIDTH), fix),
        ],
        out_specs=pl.BlockSpec((seq, POOL_WIDTH + MEM_WIDTH), row),
        out_shape=jax.ShapeDtypeStruct((n, POOL_WIDTH + MEM_WIDTH), F32),
        scratch_shapes=[
            pltpu.VMEM((mem_len, 2 * MEM_WIDTH), BF16),
            pltpu.VMEM((seq, LANES), BF16),
            pltpu.VMEM((seq, LANES), BF16),
        ],
        compiler_params=_cparams(("parallel",)),
        name="pool_memory",
    )(u, qm, mem2d, wkv_bf, wpool_bd, pscale)


def _layer_norm(z, g, b):
    mu = jnp.mean(z, axis=-1, keepdims=True)
    zc = z - mu
    var = jnp.mean(zc * zc, axis=-1, keepdims=True)
    return zc * lax.rsqrt(var + LN_EPS) * g + b


def _store_token_rows(dst_ref, val):
    tm = val.shape[0]
    for s in range(ROW_VREGS):
        dst_ref[pl.ds(s, tm, stride=ROW_VREGS), :] = val[:, s * LANES:(s + 1) * LANES]


def _load_token_rows(src_ref, tm):
    return jnp.concatenate(
        [src_ref[pl.ds(s, tm, stride=ROW_VREGS), :] for s in range(ROW_VREGS)], axis=1)


def _route(x1, wrh, wrl, rbias, ridx_ref, rw_ref):
    tm = x1.shape[0]
    xh = x1.astype(BF16)
    xl = (x1 - xh.astype(F32)).astype(BF16)
    nt = (((1,), (1,)), ((), ()))
    logits = (lax.dot_general(wrh, xh, nt, preferred_element_type=F32)
              + lax.dot_general(wrh, xl, nt, preferred_element_type=F32)
              + lax.dot_general(wrl, xh, nt, preferred_element_type=F32))
    scores = jax.nn.sigmoid(logits)
    biased = scores + rbias

    per = N_EXPERTS // N_GROUPS
    io8 = lax.broadcasted_iota(jnp.int32, (per, tm), 0)
    gscores = []
    for g in range(N_GROUPS):
        vg = biased[g * per:(g + 1) * per]
        m1 = jnp.max(vg, axis=0, keepdims=True)
        i1 = jnp.min(jnp.where(vg == m1, io8, per), axis=0, keepdims=True)
        m2 = jnp.max(jnp.where(io8 == i1, -jnp.inf, vg), axis=0, keepdims=True)
        gscores.append(m1 + m2)
    gs = jnp.concatenate(gscores, axis=0)
    iog = lax.broadcasted_iota(jnp.int32, (N_GROUPS, tm), 0)
    grank = jnp.zeros((N_GROUPS, tm), F32)
    for g in range(N_GROUPS):
        sg = gs[g:g + 1]
        tie = jnp.where(iog > g, 1.0, 0.0)
        grank = grank + jnp.where(sg > gs, 1.0, jnp.where(sg == gs, tie, 0.0))
    gsel = grank < TOPK_GROUPS
    masked = jnp.concatenate(
        [jnp.where(gsel[g:g + 1], biased[g * per:(g + 1) * per], -jnp.inf) for g in range(N_GROUPS)], axis=0)

    ioe = lax.broadcasted_iota(jnp.int32, (N_EXPERTS, tm), 0)
    rank = jnp.zeros((N_EXPERTS, tm), F32)
    for e in range(N_EXPERTS):
        me = masked[e:e + 1]
        tie = jnp.where(ioe > e, 1.0, 0.0)
        rank = rank + jnp.where(me > masked, 1.0, jnp.where(me == masked, tie, 0.0))

    idx_rows, w_rows = [], []
    for kk in range(TOP_K):
        hit = rank == float(kk)
        idx_rows.append(jnp.sum(jnp.where(hit, ioe, 0), axis=0, keepdims=True))
        w_rows.append(jnp.sum(jnp.where(hit, scores, 0.0), axis=0, keepdims=True))
    top_w = jnp.concatenate(w_rows, axis=0)
    denom = jnp.sum(top_w, axis=0, keepdims=True)
    ridx_ref[...] = jnp.concatenate(idx_rows, axis=0)
    rw_ref[...] = top_w / denom * ROUTED_SCALE


def _outproj_kernel(alpha, x_ref, ypm_ref, ydsa_ref, wo_ref, g_ref, b_ref, wrh_ref, wrl_ref, rb_ref,
                    x1_ref, x1g_ref, ridx_ref, rw_ref):
    y = jnp.concatenate([ypm_ref[:, :POOL_WIDTH], ydsa_ref[...], ypm_ref[:, POOL_WIDTH:]], axis=1)
    mix = jnp.dot(y.astype(BF16), wo_ref[...], preferred_element_type=F32)
    x1 = _layer_norm(alpha * x_ref[...] + mix, g_ref[...], b_ref[...])
    x1_ref[...] = x1
    _store_token_rows(x1g_ref, x1)
    _route(x1, wrh_ref[...], wrl_ref[...], rb_ref[...], ridx_ref, rw_ref)


def _out_projection(x2d, ypm, ydsa, wo_bf, g, b, wrh, wrl, rbias, alpha):
    n = x2d.shape[0]
    tm = TM_PROJ
    row = lambda i: (i, 0)
    fix = lambda i: (0, 0)
    col = lambda i: (0, i)
    return pl.pallas_call(
        functools.partial(_outproj_kernel, alpha),
        grid=(n // tm,),
        in_specs=[
            pl.BlockSpec((tm, D_MODEL), row),
            pl.BlockSpec((tm, POOL_WIDTH + MEM_WIDTH), row),
            pl.BlockSpec((tm, DSA_WIDTH), row),
            pl.BlockSpec((D_MODEL, D_MODEL), fix),
            pl.BlockSpec((1, D_MODEL), fix),
            pl.BlockSpec((1, D_MODEL), fix),
            pl.BlockSpec((N_EXPERTS, D_MODEL), fix),
            pl.BlockSpec((N_EXPERTS, D_MODEL), fix),
            pl.BlockSpec((N_EXPERTS, 1), fix),
        ],
        out_specs=[
            pl.BlockSpec((tm, D_MODEL), row),
            pl.BlockSpec((tm * ROW_VREGS, LANES), row),
            pl.BlockSpec((TOP_K, tm), col),
            pl.BlockSpec((TOP_K, tm), col),
        ],
        out_shape=[
            jax.ShapeDtypeStruct((n, D_MODEL), F32),
            jax.ShapeDtypeStruct((n * ROW_VREGS, LANES), F32),
            jax.ShapeDtypeStruct((TOP_K, n), jnp.int32),
            jax.ShapeDtypeStruct((TOP_K, n), F32),
        ],
        compiler_params=_cparams(("parallel",)),
        name="out_projection_router",
    )(x2d, ypm, ydsa, wo_bf, g, b, wrh, wrl, rbias)


def _moe_blocks(chunk):
    return chunk * TOP_K // MOE_BM + N_EXPERTS


def _as_column(w_row):
    bm = w_row.shape[1]
    eye = lax.broadcasted_iota(jnp.int32, (bm, bm), 0) == lax.broadcasted_iota(jnp.int32, (bm, bm), 1)
    return jnp.sum(jnp.where(eye, w_row, 0.0), axis=1, keepdims=True)


def _moe_kernel(eb_ref, nact_ref, tok_ref, wrow_ref, xg_ref, wg_ref, wu_ref, wd_ref, acc_ref,
                rows0, rows1, y0, y1):
    c = pl.program_id(0)
    j = pl.program_id(1)

    @pl.when(j == 0)
    def _():
        acc_ref[...] = jnp.zeros(acc_ref.shape, F32)

    @pl.when(jnp.logical_and(c == 0, j == 0))
    def _():
        for buf in (rows0, rows1, y0, y1):
            buf[...] = jnp.zeros(buf.shape, F32)

    def step(rows_w, rows_r, y_w, y_r):
        scatter0 = j * MOE_BM
        gather0 = scatter0 + MOE_LAG_SCATTER * MOE_BM
        for i in range(MOE_BM):
            src = pl.multiple_of(tok_ref[gather0 + i], ROW_VREGS)
            rows_w[i * ROW_VREGS:(i + 1) * ROW_VREGS, :] = xg_ref[pl.ds(src, ROW_VREGS), :]

        for g0 in range(0, MOE_BM, MOE_UNROLL):
            upd = []
            for i in range(g0, g0 + MOE_UNROLL):
                dst = pl.multiple_of(tok_ref[scatter0 + i], ROW_VREGS)
                upd.append((dst, acc_ref[pl.ds(dst, ROW_VREGS), :] + y_r[i * ROW_VREGS:(i + 1) * ROW_VREGS, :]))
            for dst, val in reversed(upd):
                acc_ref[pl.ds(dst, ROW_VREGS), :] = val

        xb = _load_token_rows(rows_r, MOE_BM).astype(BF16)
        gg = jnp.dot(xb, wg_ref[...], preferred_element_type=F32)
        uu = jnp.dot(xb, wu_ref[...], preferred_element_type=F32)
        hh = gg * jax.nn.sigmoid(gg) * uu * _as_column(wrow_ref[pl.ds(j, 1), :])
        yy = jnp.dot(hh.astype(BF16), wd_ref[...], preferred_element_type=F32)
        _store_token_rows(y_w, yy)

    active = j < nact_ref[c] + MOE_EXTRA_STEPS

    @pl.when(jnp.logical_and(active, j % 2 == 0))
    def _():
        step(rows0, rows1, y1, y0)

    @pl.when(jnp.logical_and(active, j % 2 == 1))
    def _():
        step(rows1, rows0, y0, y1)


def _routed_experts(x1g, chunk_tok, chunk_w, block_expert, n_active, w_gate, w_up, w_down, layer, chunk):
    n_tok = x1g.shape[0] // ROW_VREGS
    n_chunks = n_tok // chunk
    nb = _moe_blocks(chunk)
    steps = nb + MOE_EXTRA_STEPS
    acc_rows = chunk * ROW_VREGS

    def expert_w(c, j, eb, na):
        return (layer, eb[c * nb + jnp.clip(j - MOE_LAG_EXPERT, 0, nb - 1)], 0, 0)

    per_chunk = lambda c, j, eb, na: (c, 0, 0)
    rows = lambda: pltpu.VMEM((MOE_BM * ROW_VREGS, LANES), F32)
    grid_spec = pltpu.PrefetchScalarGridSpec(
        num_scalar_prefetch=2,
        grid=(n_chunks, steps),
        in_specs=[
            pl.BlockSpec((chunk_tok.shape[0] // n_chunks,), lambda c, j, eb, na: (c,), memory_space=pltpu.SMEM),
            pl.BlockSpec((None,) + chunk_w.shape[1:], per_chunk),
            pl.BlockSpec((chunk * ROW_VREGS, LANES), lambda c, j, eb, na: (c, 0), pipeline_mode=pl.Buffered(1)),
            pl.BlockSpec((None, None, D_MODEL, D_EXPERT), expert_w),
            pl.BlockSpec((None, None, D_MODEL, D_EXPERT), expert_w),
            pl.BlockSpec((None, None, D_EXPERT, D_MODEL), expert_w),
        ],
        out_specs=pl.BlockSpec((None, acc_rows, LANES), per_chunk, pipeline_mode=pl.Buffered(1)),
        scratch_shapes=[rows(), rows(), rows(), rows()],
    )
    return pl.pallas_call(
        _moe_kernel,
        grid_spec=grid_spec,
        out_shape=jax.ShapeDtypeStruct((n_chunks, acc_rows, LANES), F32),
        compiler_params=_cparams(("arbitrary", "arbitrary")),
        name="routed_experts",
    )(block_expert, n_active, chunk_tok, chunk_w, x1g, w_gate, w_up, w_down)


def _shared_kernel(alpha, x_ref, acc_ref, wg_ref, wu_ref, wd_ref, g_ref, b_ref, o_ref):
    x1 = x_ref[...]
    tm = x1.shape[0]
    xb = x1.astype(BF16)
    gg = jnp.dot(xb, wg_ref[...], preferred_element_type=F32)
    uu = jnp.dot(xb, wu_ref[...], preferred_element_type=F32)
    hh = (gg * jax.nn.sigmoid(gg) * uu).astype(BF16)
    shared = jnp.dot(hh, wd_ref[...], preferred_element_type=F32)
    routed = _load_token_rows(acc_ref, tm)
    o_ref[...] = _layer_norm(alpha * x1 + (routed + shared), g_ref[...], b_ref[...])


def _shared_and_norm(x1, acc, wsg, wsu, wsd, g, b, alpha, chunk):
    n = x1.shape[0]
    tm = TM_PROJ
    per_chunk = chunk // tm
    row = lambda i: (i, 0)
    fix = lambda i: (0, 0)
    return pl.pallas_call(
        functools.partial(_shared_kernel, alpha),
        grid=(n // tm,),
        in_specs=[
            pl.BlockSpec((tm, D_MODEL), row),
            pl.BlockSpec((None, tm * ROW_VREGS, LANES), lambda i: (i // per_chunk, i % per_chunk, 0)),
            pl.BlockSpec((D_MODEL, D_EXPERT), fix),
            pl.BlockSpec((D_MODEL, D_EXPERT), fix),
            pl.BlockSpec((D_EXPERT, D_MODEL), fix),
            pl.BlockSpec((1, D_MODEL), fix),
            pl.BlockSpec((1, D_MODEL), fix),
        ],
        out_specs=pl.BlockSpec((tm, D_MODEL), row),
        out_shape=jax.ShapeDtypeStruct((n, D_MODEL), F32),
        compiler_params=_cparams(("parallel",)),
        name="shared_expert_norm",
    )(x1, acc, wsg, wsu, wsd, g, b)


def _rope_tables(seq):
    half = ROT_DIM // 2
    pos = jnp.arange(seq, dtype=F32)
    inv = ROPE_THETA ** (-jnp.arange(0, ROT_DIM, 2, dtype=F32) / ROT_DIM)
    ang = pos[:, None] * inv[None, :]
    cos, sin = jnp.cos(ang), jnp.sin(ang)
    pad = HEAD_DIM - ROT_DIM
    ones = jnp.ones((seq, pad), F32)
    zeros = jnp.zeros((seq, pad), F32)
    zh = jnp.zeros((seq, half), F32)
    c_head = jnp.concatenate([cos, cos, ones], axis=1)
    s1_head = jnp.concatenate([-sin, zh, zeros], axis=1)
    s2_head = jnp.concatenate([zh, sin, zeros], axis=1)
    rep = LANES // HEAD_DIM
    return (jnp.tile(c_head, (1, rep)), jnp.tile(s1_head, (1, rep)), jnp.tile(s2_head, (1, rep)))


def _dispatch_lists(ridx, rw, chunk):
    n = ridx.shape[1]
    n_chunks = n // chunk
    e_flat = ridx.T.reshape(n_chunks, chunk * TOP_K)
    w_flat = rw.T.reshape(n_chunks, chunk * TOP_K)
    n_asg = chunk * TOP_K
    nb = _moe_blocks(chunk)
    pos = lax.broadcasted_iota(jnp.int32, e_flat.shape, 1)
    _, order = lax.sort((e_flat, pos), dimension=1, is_stable=True, num_keys=1)
    experts = jnp.arange(N_EXPERTS, dtype=jnp.int32)
    counts = jnp.sum((e_flat[:, :, None] == experts).astype(jnp.int32), axis=1)
    start = jnp.cumsum(counts, axis=1) - counts
    n_blk = (counts + MOE_BM - 1) // MOE_BM
    b_end = jnp.cumsum(n_blk, axis=1)
    b_start = b_end - n_blk
    n_active = b_end[:, -1]

    jb = jnp.arange(nb, dtype=jnp.int32)
    eb = jnp.minimum(jnp.sum((jb[None, :, None] >= b_end[:, None, :]).astype(jnp.int32), axis=2), N_EXPERTS - 1)
    take = lambda a: jnp.take_along_axis(a, eb, axis=1)
    row = (jb[None, :] - take(b_start))[:, :, None] * MOE_BM + jnp.arange(MOE_BM, dtype=jnp.int32)
    valid = (jb[None, :, None] < n_active[:, None, None]) & (row < take(counts)[:, :, None])
    src = jnp.clip(take(start)[:, :, None] + row, 0, n_asg - 1).reshape(n_chunks, -1)
    asg = jnp.take_along_axis(order, src, axis=1)
    tok = jnp.where(valid, (asg // TOP_K).reshape(valid.shape), 0) * ROW_VREGS
    wgt = jnp.where(valid, jnp.take_along_axis(w_flat, asg, axis=1).reshape(valid.shape), 0.0)

    def padded(a, before, after):
        pad = lambda k: jnp.zeros((n_chunks, k, MOE_BM), a.dtype)
        return jnp.concatenate([pad(before), a, pad(after)], axis=1)
    tok_blocks = MOE_LAG_SCATTER + nb + MOE_EXTRA_STEPS
    align = SMEM_BLOCK_1D // MOE_BM
    tail = MOE_EXTRA_STEPS + (-tok_blocks) % align
    chunk_tok = padded(tok, MOE_LAG_SCATTER, tail).reshape(-1)
    chunk_w = padded(wgt, MOE_LAG_EXPERT, MOE_EXTRA_STEPS - MOE_LAG_EXPERT)
    return chunk_tok, chunk_w, eb.reshape(-1), n_active


def kernel(x, mem, w_in, w_pool, pool_scale, w_mem_kv, w_out, ln1_g, ln1_b, w_router, router_bias,
           w_gate, w_up, w_down, ws_gate, ws_up, ws_down, ln2_g, ln2_b):
    batch, seq, d = x.shape
    depth = w_in.shape[0]
    assert d == D_MODEL and seq % (16 * QBLK) == 0 and seq % TM_PROJ == 0
    assert (seq // QBLK - 1) % DSA_UNROLL1 == 0 and (seq // QBLK) % max(DSA_UNROLL16, MEM_UNROLL) == 0
    n = batch * seq
    chunk = min(MOE_CHUNK, n)
    assert n % chunk == 0 and chunk % TM_PROJ == 0
    alpha = float((2 * depth) ** 0.25)

    rope_c, rope_s1, rope_s2 = _rope_tables(seq)
    wg_bf, wu_bf, wd_bf = w_gate.astype(BF16), w_up.astype(BF16), w_down.astype(BF16)
    x2d = x.reshape(n, d)
    mem2d = mem.reshape(batch * mem.shape[1], d)

    for l in range(depth):
        w_in_bf = w_in[l].astype(BF16)
        wkv_bf = w_mem_kv[l].astype(BF16)
        wo_bf = w_out[l].astype(BF16)
        wpool_bd = jax.scipy.linalg.block_diag(*[w_pool[l, g] for g in range(len(POOL_WINDOWS))]).astype(BF16)
        pscale = pool_scale[l].reshape(1, POOL_WIDTH)
        wr_t = w_router[l].T
        wrh = wr_t.astype(BF16)
        wrl = (wr_t - wrh.astype(F32)).astype(BF16)
        rbias = router_bias[l].reshape(N_EXPERTS, 1)

        u, q, k, v, qm = _in_projection(x2d, w_in_bf, rope_c, rope_s1, rope_s2, seq)
        ydsa = _dilated_attention(q, k, v, batch, seq)
        ypm = _pool_and_memory(u, qm, mem2d, wkv_bf, wpool_bd, pscale, batch, seq)
        x1, x1g, ridx, rw = _out_projection(
            x2d, ypm, ydsa, wo_bf, ln1_g[l].reshape(1, d), ln1_b[l].reshape(1, d), wrh, wrl, rbias, alpha)
        chunk_tok, chunk_w, block_expert, n_active = _dispatch_lists(ridx, rw, chunk)
        acc = _routed_experts(x1g, chunk_tok, chunk_w, block_expert, n_active, wg_bf, wu_bf, wd_bf, l, chunk)
        x2d = _shared_and_norm(
            x1, acc, ws_gate[l].astype(BF16), ws_up[l].astype(BF16), ws_down[l].astype(BF16),
            ln2_g[l].reshape(1, d), ln2_b[l].reshape(1, d), alpha, chunk)
    return x2d.reshape(batch, seq, d)
```

```python
import functools

import jax
import jax.numpy as jnp
from jax import lax
from jax.experimental import pallas as pl
from jax.experimental.pallas import tpu as pltpu

F32 = jnp.float32
BF16 = jnp.bfloat16

D_MODEL = 1024
HEAD_DIM = 64
POOL_WINDOWS = (2, 4, 8, 16)
POOL_WIDTH = 256
POOL_GROUP = 64
DSA_WIDTH = 512
DSA_LOOKBACK = 128
DSA_DILATIONS = (1, 4, 16)
MEM_WIDTH = 256
IN_WIDTH = 2048
ROPE_THETA = 500000.0
ROT_DIM = 16
N_EXPERTS = 64
TOP_K = 8
N_GROUPS = 8
TOPK_GROUPS = 4
D_EXPERT = 256
ROUTED_SCALE = 2.5
LN_EPS = 1e-5

LANES = 128
SUBLANES = 8
SMEM_BLOCK_1D = 1024
ROW_VREGS = D_MODEL // LANES

NEG_BIG = -1e30

TM_PROJ = 512
QBLK = 128
DSA_UNROLL1 = 15
DSA_RES4 = 4
DSA_UNROLL16 = 16
MEM_UNROLL = 16
MOE_CHUNK = 4096
MOE_BM = 256
MOE_UNROLL = 8
MOE_SPLIT = 1
MOE_LAG_EXPERT = 1
MOE_LAG_SCATTER = 2
MOE_EXTRA_STEPS = MOE_LAG_SCATTER + 1
VMEM_LIMIT = 56 * 1024 * 1024


def _cparams(sem):
    return pltpu.CompilerParams(dimension_semantics=sem, vmem_limit_bytes=VMEM_LIMIT)


def _inproj_kernel(x_ref, w_ref, c_ref, s1_ref, s2_ref, u_ref, q_ref, k_ref, v_ref, qm_ref):
    xb = x_ref[...].astype(BF16)
    h = jnp.dot(xb, w_ref[...], preferred_element_type=F32)
    c = c_ref[...]
    s1 = s1_ref[...]
    s2 = s2_ref[...]

    def rot(a):
        return a * c + pltpu.roll(a, LANES - ROT_DIM // 2, 1) * s1 + pltpu.roll(a, ROT_DIM // 2, 1) * s2

    scale = HEAD_DIM ** -0.5
    u_ref[...] = h[:, :POOL_WIDTH]
    q0 = POOL_WIDTH
    k0 = q0 + DSA_WIDTH
    v0 = k0 + DSA_WIDTH
    m0 = v0 + DSA_WIDTH
    for j in range(DSA_WIDTH // LANES):
        q_ref[:, j * LANES:(j + 1) * LANES] = rot(h[:, q0 + j * LANES:q0 + (j + 1) * LANES]) * scale
        k_ref[:, j * LANES:(j + 1) * LANES] = rot(h[:, k0 + j * LANES:k0 + (j + 1) * LANES])
    v_ref[...] = h[:, v0:m0]
    qm_ref[...] = h[:, m0:] * scale


def _in_projection(x2d, w_in_bf, rope_c, rope_s1, rope_s2, seq):
    n = x2d.shape[0]
    tm = TM_PROJ
    per_seq = seq // tm
    row = lambda i: (i, 0)
    tab = lambda i: (i % per_seq, 0)
    return pl.pallas_call(
        _inproj_kernel,
        grid=(n // tm,),
        in_specs=[
            pl.BlockSpec((tm, D_MODEL), row),
            pl.BlockSpec((D_MODEL, IN_WIDTH), lambda i: (0, 0)),
            pl.BlockSpec((tm, LANES), tab),
            pl.BlockSpec((tm, LANES), tab),
            pl.BlockSpec((tm, LANES), tab),
        ],
        out_specs=[
            pl.BlockSpec((tm, POOL_WIDTH), row),
            pl.BlockSpec((tm, DSA_WIDTH), row),
            pl.BlockSpec((tm, DSA_WIDTH), row),
            pl.BlockSpec((tm, DSA_WIDTH), row),
            pl.BlockSpec((tm, MEM_WIDTH), row),
        ],
        out_shape=[
            jax.ShapeDtypeStruct((n, POOL_WIDTH), F32),
            jax.ShapeDtypeStruct((n, DSA_WIDTH), F32),
            jax.ShapeDtypeStruct((n, DSA_WIDTH), F32),
            jax.ShapeDtypeStruct((n, DSA_WIDTH), F32),
            jax.ShapeDtypeStruct((n, MEM_WIDTH), F32),
        ],
        compiler_params=_cparams(("parallel",)),
        name="in_projection",
    )(x2d, w_in_bf, rope_c, rope_s1, rope_s2)


def _two_head_attention(q2, kk, vv, bias):
    s = lax.dot_general(q2, kk, (((1,), (1,)), ((), ())), preferred_element_type=F32)
    if bias is not None:
        s = s + bias
    m = jnp.max(s, axis=-1, keepdims=True)
    p = jnp.exp(s - m)
    l = jnp.sum(p, axis=-1, keepdims=True)
    o2 = jnp.dot(p.astype(BF16), vv, preferred_element_type=F32) / l
    lse2 = m + jnp.log(l)
    first = lax.broadcasted_iota(jnp.int32, (QBLK, LANES), 1) < HEAD_DIM
    o = jnp.where(first, o2[:QBLK], o2[QBLK:])
    lse = jnp.where(first, lse2[:QBLK], lse2[QBLK:])
    return o, lse


def _band_bias(nk, dbase):
    qi = lax.broadcasted_iota(jnp.int32, (2 * QBLK, nk), 0) & (QBLK - 1)
    kj = lax.broadcasted_iota(jnp.int32, (2 * QBLK, nk), 1)
    diff = qi - kj + dbase
    ok = lax.bitcast_convert_type(diff, jnp.uint32) <= jnp.uint32(DSA_LOOKBACK)
    return jnp.where(ok, 0.0, NEG_BIG).astype(F32)


def _dsa_kernel(q_ref, k_ref, v_ref, o_ref,
                qa1, qb1, k1, v1, qa4, qb4, k4, v4, qa16, qb16, k16, v16,
                o1, l1, o4, l4, o16, l16):
    seq = q_ref.shape[0]

    def stage(dil, qa, qb, kd, vd):
        sub = seq // dil
        for r in range(dil):
            if dil == 1:
                qv, kv, vv = q_ref[...], k_ref[...], v_ref[...]
            else:
                qv = q_ref[pl.ds(r, sub, stride=dil), :]
                kv = k_ref[pl.ds(r, sub, stride=dil), :]
                vv = v_ref[pl.ds(r, sub, stride=dil), :]
            fm = lax.broadcasted_iota(jnp.int32, (sub, LANES), 1) < HEAD_DIM
            qa[r * sub:(r + 1) * sub, :] = jnp.where(fm, qv, 0.0).astype(BF16)
            qb[r * sub:(r + 1) * sub, :] = jnp.where(fm, 0.0, qv).astype(BF16)
            kd[r * sub:(r + 1) * sub, :] = kv.astype(BF16)
            vd[r * sub:(r + 1) * sub, :] = vv.astype(BF16)

    stage(1, qa1, qb1, k1, v1)
    stage(4, qa4, qb4, k4, v4)
    stage(16, qa16, qb16, k16, v16)

    n_blocks = seq // QBLK
    causal = _band_bias(QBLK, 0)
    banded = _band_bias(2 * QBLK, QBLK)

    def block(bufs, qrow, with_prev):
        qa, qb, kd, vd, od, ld = bufs
        qrow = pl.multiple_of(qrow, QBLK)
        q2 = jnp.concatenate([qa[pl.ds(qrow, QBLK), :], qb[pl.ds(qrow, QBLK), :]], axis=0)
        if with_prev:
            krow, nk, bias = pl.multiple_of(qrow - QBLK, QBLK), 2 * QBLK, banded
        else:
            krow, nk, bias = qrow, QBLK, causal
        o, lse = _two_head_attention(q2, kd[pl.ds(krow, nk), :], vd[pl.ds(krow, nk), :], bias)
        od[pl.ds(qrow, QBLK), :] = o
        ld[pl.ds(qrow, QBLK), :] = lse

    bufs1 = (qa1, qb1, k1, v1, o1, l1)
    block(bufs1, 0, False)

    def body1(it, carry):
        for u in range(DSA_UNROLL1):
            block(bufs1, (1 + it * DSA_UNROLL1 + u) * QBLK, True)
        return carry
    lax.fori_loop(0, (n_blocks - 1) // DSA_UNROLL1, body1, 0)

    bufs4 = (qa4, qb4, k4, v4, o4, l4)
    per4 = n_blocks // 4

    def body4(it, carry):
        for rr in range(DSA_RES4):
            base = (it * DSA_RES4 + rr) * per4 * QBLK
            block(bufs4, base, False)
            for u in range(1, per4):
                block(bufs4, base + u * QBLK, True)
        return carry
    lax.fori_loop(0, 4 // DSA_RES4, body4, 0)

    bufs16 = (qa16, qb16, k16, v16, o16, l16)

    def body16(it, carry):
        for u in range(DSA_UNROLL16):
            block(bufs16, (it * DSA_UNROLL16 + u) * QBLK, False)
        return carry
    lax.fori_loop(0, n_blocks // DSA_UNROLL16, body16, 0)

    sub16 = seq // 16
    sub4 = seq // 4
    for r in range(16):
        nat = pl.ds(r, sub16, stride=16)
        res4 = pl.ds((r % 4) * sub4 + r // 4, sub16, stride=4)
        res16 = pl.ds(r * sub16, sub16)
        a_l, b_l, c_l = l1[nat, :], l4[res4, :], l16[res16, :]
        m = jnp.maximum(jnp.maximum(a_l, b_l), c_l)
        ea = jnp.exp(a_l - m)
        eb = jnp.exp(b_l - m)
        ec = jnp.exp(c_l - m)
        out = (ea * o1[nat, :] + eb * o4[res4, :] + ec * o16[res16, :]) / (ea + eb + ec)
        o_ref[nat, :] = out


def _dilated_attention(q, k, v, batch, seq):
    n = q.shape[0]
    pairs = DSA_WIDTH // LANES
    blk = pl.BlockSpec((seq, LANES), lambda b, j: (b, j))
    bf = lambda: pltpu.VMEM((seq, LANES), BF16)
    ff = lambda: pltpu.VMEM((seq, LANES), F32)
    return pl.pallas_call(
        _dsa_kernel,
        grid=(batch, pairs),
        in_specs=[blk, blk, blk],
        out_specs=blk,
        out_shape=jax.ShapeDtypeStruct((n, DSA_WIDTH), F32),
        scratch_shapes=[bf() for _ in range(12)] + [ff() for _ in range(6)],
        compiler_params=_cparams(("parallel", "parallel")),
        name="dilated_attention",
    )(q, k, v)


def _poolmem_kernel(u_ref, qm_ref, mem_ref, wkv_ref, wpool_ref, pscale_ref, y_ref, kv_s, qa_s, qb_s):
    seq = u_ref.shape[0]
    u = u_ref[...]
    rows = lax.broadcasted_iota(jnp.int32, (seq, POOL_WIDTH), 0)
    grp = lax.broadcasted_iota(jnp.int32, (seq, POOL_WIDTH), 1) // POOL_GROUP

    def shifted(a, kk):
        return jnp.where(rows >= kk, pltpu.roll(a, kk, 0), 0.0)

    s2 = u + shifted(u, 1)
    s4 = s2 + shifted(s2, 2)
    s8 = s4 + shifted(s4, 4)
    s16 = s8 + shifted(s8, 8)
    wsum = jnp.where(grp == 0, s2, jnp.where(grp == 1, s4, jnp.where(grp == 2, s8, s16)))
    win = jnp.where(grp == 0, 2, jnp.where(grp == 1, 4, jnp.where(grp == 2, 8, 16)))
    cnt = jnp.minimum(rows + 1, win).astype(F32)
    pooled = wsum / cnt - u
    mixed = jnp.dot(pooled.astype(BF16), wpool_ref[...], preferred_element_type=F32) * pscale_ref[...]
    y_ref[:, :POOL_WIDTH] = mixed

    kv_s[...] = jnp.dot(mem_ref[...].astype(BF16), wkv_ref[...], preferred_element_type=F32).astype(BF16)
    first = lax.broadcasted_iota(jnp.int32, (seq, LANES), 1) < HEAD_DIM
    for j in range(MEM_WIDTH // LANES):
        qv = qm_ref[:, j * LANES:(j + 1) * LANES]
        qa_s[...] = jnp.where(first, qv, 0.0).astype(BF16)
        qb_s[...] = jnp.where(first, 0.0, qv).astype(BF16)
        kk = kv_s[:, j * LANES:(j + 1) * LANES]
        vv = kv_s[:, MEM_WIDTH + j * LANES:MEM_WIDTH + (j + 1) * LANES]

        def body(it, carry):
            for u in range(MEM_UNROLL):
                qrow = pl.multiple_of((it * MEM_UNROLL + u) * QBLK, QBLK)
                q2 = jnp.concatenate([qa_s[pl.ds(qrow, QBLK), :], qb_s[pl.ds(qrow, QBLK), :]], axis=0)
                o, _ = _two_head_attention(q2, kk, vv, None)
                y_ref[pl.ds(qrow, QBLK), pl.ds(POOL_WIDTH + j * LANES, LANES)] = o
            return carry
        lax.fori_loop(0, seq // (QBLK * MEM_UNROLL), body, 0)


def _pool_and_memory(u, qm, mem2d, wkv_bf, wpool_bd, pscale, batch, seq):
    n = u.shape[0]
    mem_len = mem2d.shape[0] // batch
    row = lambda b: (b, 0)
    fix = lambda b: (0, 0)
    return pl.pallas_call(
        _poolmem_kernel,
        grid=(batch,),
        in_specs=[
            pl.BlockSpec((seq, POOL_WIDTH), row),
            pl.BlockSpec((seq, MEM_WIDTH), row),
            pl.BlockSpec((mem_len, D_MODEL), row),
            pl.BlockSpec((D_MODEL, 2 * MEM_WIDTH), fix),
            pl.BlockSpec((POOL_WIDTH, POOL_WIDTH), fix),
            pl.BlockSpec((1, POOL_WIDTH), fix),
        ],
        out_specs=pl.BlockSpec((seq, POOL_WIDTH + MEM_WIDTH), row),
        out_shape=jax.ShapeDtypeStruct((n, POOL_WIDTH + MEM_WIDTH), F32),
        scratch_shapes=[
            pltpu.VMEM((mem_len, 2 * MEM_WIDTH), BF16),
            pltpu.VMEM((seq, LANES), BF16),
            pltpu.VMEM((seq, LANES), BF16),
        ],
        compiler_params=_cparams(("parallel",)),
        name="pool_memory",
    )(u, qm, mem2d, wkv_bf, wpool_bd, pscale)


def _layer_norm(z, g, b):
    mu = jnp.mean(z, axis=-1, keepdims=True)
    zc = z - mu
    var = jnp.mean(zc * zc, axis=-1, keepdims=True)
    return zc * lax.rsqrt(var + LN_EPS) * g + b


def _store_token_rows(dst_ref, val, first=0):
    tm = val.shape[0]
    for s in range(ROW_VREGS):
        dst_ref[pl.ds(first * ROW_VREGS + s, tm, stride=ROW_VREGS), :] = val[:, s * LANES:(s + 1) * LANES]


def _load_token_rows(src_ref, tm, first=0):
    return jnp.concatenate(
        [src_ref[pl.ds(first * ROW_VREGS + s, tm, stride=ROW_VREGS), :] for s in range(ROW_VREGS)], axis=1)


def _route(x1, wrh, wrl, rbias, ridx_ref, rw_ref):
    tm = x1.shape[0]
    xh = x1.astype(BF16)
    xl = (x1 - xh.astype(F32)).astype(BF16)
    nt = (((1,), (1,)), ((), ()))
    logits = (lax.dot_general(wrh, xh, nt, preferred_element_type=F32)
              + lax.dot_general(wrh, xl, nt, preferred_element_type=F32)
              + lax.dot_general(wrl, xh, nt, preferred_element_type=F32))
    scores = jax.nn.sigmoid(logits)
    biased = scores + rbias

    per = N_EXPERTS // N_GROUPS
    io8 = lax.broadcasted_iota(jnp.int32, (per, tm), 0)
    gscores = []
    for g in range(N_GROUPS):
        vg = biased[g * per:(g + 1) * per]
        m1 = jnp.max(vg, axis=0, keepdims=True)
        i1 = jnp.min(jnp.where(vg == m1, io8, per), axis=0, keepdims=True)
        m2 = jnp.max(jnp.where(io8 == i1, -jnp.inf, vg), axis=0, keepdims=True)
        gscores.append(m1 + m2)
    gs = jnp.concatenate(gscores, axis=0)
    iog = lax.broadcasted_iota(jnp.int32, (N_GROUPS, tm), 0)
    grank = jnp.zeros((N_GROUPS, tm), F32)
    for g in range(N_GROUPS):
        sg = gs[g:g + 1]
        tie = jnp.where(iog > g, 1.0, 0.0)
        grank = grank + jnp.where(sg > gs, 1.0, jnp.where(sg == gs, tie, 0.0))
    gsel = grank < TOPK_GROUPS
    masked = jnp.concatenate(
        [jnp.where(gsel[g:g + 1], biased[g * per:(g + 1) * per], -jnp.inf) for g in range(N_GROUPS)], axis=0)

    ioe = lax.broadcasted_iota(jnp.int32, (N_EXPERTS, tm), 0)
    rank = jnp.zeros((N_EXPERTS, tm), F32)
    for e in range(N_EXPERTS):
        me = masked[e:e + 1]
        tie = jnp.where(ioe > e, 1.0, 0.0)
        rank = rank + jnp.where(me > masked, 1.0, jnp.where(me == masked, tie, 0.0))

    idx_rows, w_rows = [], []
    for kk in range(TOP_K):
        hit = rank == float(kk)
        idx_rows.append(jnp.sum(jnp.where(hit, ioe, 0), axis=0, keepdims=True))
        w_rows.append(jnp.sum(jnp.where(hit, scores, 0.0), axis=0, keepdims=True))
    top_w = jnp.concatenate(w_rows, axis=0)
    denom = jnp.sum(top_w, axis=0, keepdims=True)
    ridx_ref[...] = jnp.concatenate(idx_rows, axis=0)
    rw_ref[...] = top_w / denom * ROUTED_SCALE


def _outproj_kernel(alpha, x_ref, ypm_ref, ydsa_ref, wo_ref, g_ref, b_ref, wrh_ref, wrl_ref, rb_ref,
                    x1_ref, x1g_ref, ridx_ref, rw_ref):
    y = jnp.concatenate([ypm_ref[:, :POOL_WIDTH], ydsa_ref[...], ypm_ref[:, POOL_WIDTH:]], axis=1)
    mix = jnp.dot(y.astype(BF16), wo_ref[...], preferred_element_type=F32)
    x1 = _layer_norm(alpha * x_ref[...] + mix, g_ref[...], b_ref[...])
    x1_ref[...] = x1
    _store_token_rows(x1g_ref, x1)
    _route(x1, wrh_ref[...], wrl_ref[...], rb_ref[...], ridx_ref, rw_ref)


def _out_projection(x2d, ypm, ydsa, wo_bf, g, b, wrh, wrl, rbias, alpha):
    n = x2d.shape[0]
    tm = TM_PROJ
    row = lambda i: (i, 0)
    fix = lambda i: (0, 0)
    col = lambda i: (0, i)
    return pl.pallas_call(
        functools.partial(_outproj_kernel, alpha),
        grid=(n // tm,),
        in_specs=[
            pl.BlockSpec((tm, D_MODEL), row),
            pl.BlockSpec((tm, POOL_WIDTH + MEM_WIDTH), row),
            pl.BlockSpec((tm, DSA_WIDTH), row),
            pl.BlockSpec((D_MODEL, D_MODEL), fix),
            pl.BlockSpec((1, D_MODEL), fix),
            pl.BlockSpec((1, D_MODEL), fix),
            pl.BlockSpec((N_EXPERTS, D_MODEL), fix),
            pl.BlockSpec((N_EXPERTS, D_MODEL), fix),
            pl.BlockSpec((N_EXPERTS, 1), fix),
        ],
        out_specs=[
            pl.BlockSpec((tm, D_MODEL), row),
            pl.BlockSpec((tm * ROW_VREGS, LANES), row),
            pl.BlockSpec((TOP_K, tm), col),
            pl.BlockSpec((TOP_K, tm), col),
        ],
        out_shape=[
            jax.ShapeDtypeStruct((n, D_MODEL), F32),
            jax.ShapeDtypeStruct((n * ROW_VREGS, LANES), F32),
            jax.ShapeDtypeStruct((TOP_K, n), jnp.int32),
            jax.ShapeDtypeStruct((TOP_K, n), F32),
        ],
        compiler_params=_cparams(("parallel",)),
        name="out_projection_router",
    )(x2d, ypm, ydsa, wo_bf, g, b, wrh, wrl, rbias)


def _moe_blocks(chunk):
    return chunk * TOP_K // MOE_BM + N_EXPERTS


def _as_column(w_row):
    bm = w_row.shape[1]
    eye = lax.broadcasted_iota(jnp.int32, (bm, bm), 0) == lax.broadcasted_iota(jnp.int32, (bm, bm), 1)
    return jnp.sum(jnp.where(eye, w_row, 0.0), axis=1, keepdims=True)


def _moe_kernel(eb_ref, nact_ref, tok_ref, wrow_ref, xg_ref, wg_ref, wu_ref, wd_ref, acc_ref,
                rows0, rows1, y0, y1):
    c = pl.program_id(0)
    j = pl.program_id(1)

    @pl.when(j == 0)
    def _():
        acc_ref[...] = jnp.zeros(acc_ref.shape, F32)

    @pl.when(jnp.logical_and(c == 0, j == 0))
    def _():
        for buf in (rows0, rows1, y0, y1):
            buf[...] = jnp.zeros(buf.shape, F32)

    def step(rows_w, rows_r, y_w, y_r):
        scatter0 = j * MOE_BM
        gather0 = scatter0 + MOE_LAG_SCATTER * MOE_BM
        for i in range(MOE_BM):
            src = pl.multiple_of(tok_ref[gather0 + i], ROW_VREGS)
            rows_w[i * ROW_VREGS:(i + 1) * ROW_VREGS, :] = xg_ref[pl.ds(src, ROW_VREGS), :]

        for g0 in range(0, MOE_BM, MOE_UNROLL):
            upd = []
            for i in range(g0, g0 + MOE_UNROLL):
                dst = pl.multiple_of(tok_ref[scatter0 + i], ROW_VREGS)
                upd.append((dst, acc_ref[pl.ds(dst, ROW_VREGS), :] + y_r[i * ROW_VREGS:(i + 1) * ROW_VREGS, :]))
            for dst, val in reversed(upd):
                acc_ref[pl.ds(dst, ROW_VREGS), :] = val

        wcol = _as_column(wrow_ref[pl.ds(j, 1), :])
        sub = MOE_BM // MOE_SPLIT
        for h in range(MOE_SPLIT):
            xb = _load_token_rows(rows_r, sub, h * sub).astype(BF16)
            gg = jnp.dot(xb, wg_ref[...], preferred_element_type=F32)
            uu = jnp.dot(xb, wu_ref[...], preferred_element_type=F32)
            hh = gg * jax.nn.sigmoid(gg) * uu * wcol[h * sub:(h + 1) * sub]
            yy = jnp.dot(hh.astype(BF16), wd_ref[...], preferred_element_type=F32)
            _store_token_rows(y_w, yy, h * sub)

    active = j < nact_ref[c] + MOE_EXTRA_STEPS

    @pl.when(jnp.logical_and(active, j % 2 == 0))
    def _():
        step(rows0, rows1, y1, y0)

    @pl.when(jnp.logical_and(active, j % 2 == 1))
    def _():
        step(rows1, rows0, y0, y1)


def _routed_experts(x1g, chunk_tok, chunk_w, block_expert, n_active, w_gate, w_up, w_down, layer, chunk):
    n_tok = x1g.shape[0] // ROW_VREGS
    n_chunks = n_tok // chunk
    nb = _moe_blocks(chunk)
    steps = nb + MOE_EXTRA_STEPS
    acc_rows = chunk * ROW_VREGS

    def expert_w(c, j, eb, na):
        return (layer, eb[c * nb + jnp.clip(j - MOE_LAG_EXPERT, 0, nb - 1)], 0, 0)

    per_chunk = lambda c, j, eb, na: (c, 0, 0)
    rows = lambda: pltpu.VMEM((MOE_BM * ROW_VREGS, LANES), F32)
    grid_spec = pltpu.PrefetchScalarGridSpec(
        num_scalar_prefetch=2,
        grid=(n_chunks, steps),
        in_specs=[
            pl.BlockSpec((chunk_tok.shape[0] // n_chunks,), lambda c, j, eb, na: (c,), memory_space=pltpu.SMEM),
            pl.BlockSpec((None,) + chunk_w.shape[1:], per_chunk),
            pl.BlockSpec((chunk * ROW_VREGS, LANES), lambda c, j, eb, na: (c, 0), pipeline_mode=pl.Buffered(1)),
            pl.BlockSpec((None, None, D_MODEL, D_EXPERT), expert_w),
            pl.BlockSpec((None, None, D_MODEL, D_EXPERT), expert_w),
            pl.BlockSpec((None, None, D_EXPERT, D_MODEL), expert_w),
        ],
        out_specs=pl.BlockSpec((None, acc_rows, LANES), per_chunk, pipeline_mode=pl.Buffered(1)),
        scratch_shapes=[rows(), rows(), rows(), rows()],
    )
    return pl.pallas_call(
        _moe_kernel,
        grid_spec=grid_spec,
        out_shape=jax.ShapeDtypeStruct((n_chunks, acc_rows, LANES), F32),
        compiler_params=_cparams(("arbitrary", "arbitrary")),
        name="routed_experts",
    )(block_expert, n_active, chunk_tok, chunk_w, x1g, w_gate, w_up, w_down)


def _shared_kernel(alpha, x_ref, acc_ref, wg_ref, wu_ref, wd_ref, g_ref, b_ref, o_ref):
    x1 = x_ref[...]
    tm = x1.shape[0]
    xb = x1.astype(BF16)
    gg = jnp.dot(xb, wg_ref[...], preferred_element_type=F32)
    uu = jnp.dot(xb, wu_ref[...], preferred_element_type=F32)
    hh = (gg * jax.nn.sigmoid(gg) * uu).astype(BF16)
    shared = jnp.dot(hh, wd_ref[...], preferred_element_type=F32)
    routed = _load_token_rows(acc_ref, tm)
    o_ref[...] = _layer_norm(alpha * x1 + (routed + shared), g_ref[...], b_ref[...])


def _shared_and_norm(x1, acc, wsg, wsu, wsd, g, b, alpha, chunk):
    n = x1.shape[0]
    tm = TM_PROJ
    per_chunk = chunk // tm
    row = lambda i: (i, 0)
    fix = lambda i: (0, 0)
    return pl.pallas_call(
        functools.partial(_shared_kernel, alpha),
        grid=(n // tm,),
        in_specs=[
            pl.BlockSpec((tm, D_MODEL), row),
            pl.BlockSpec((None, tm * ROW_VREGS, LANES), lambda i: (i // per_chunk, i % per_chunk, 0)),
            pl.BlockSpec((D_MODEL, D_EXPERT), fix),
            pl.BlockSpec((D_MODEL, D_EXPERT), fix),
            pl.BlockSpec((D_EXPERT, D_MODEL), fix),
            pl.BlockSpec((1, D_MODEL), fix),
            pl.BlockSpec((1, D_MODEL), fix),
        ],
        out_specs=pl.BlockSpec((tm, D_MODEL), row),
        out_shape=jax.ShapeDtypeStruct((n, D_MODEL), F32),
        compiler_params=_cparams(("parallel",)),
        name="shared_expert_norm",
    )(x1, acc, wsg, wsu, wsd, g, b)


def _rope_tables(seq):
    half = ROT_DIM // 2
    pos = jnp.arange(seq, dtype=F32)
    inv = ROPE_THETA ** (-jnp.arange(0, ROT_DIM, 2, dtype=F32) / ROT_DIM)
    ang = pos[:, None] * inv[None, :]
    cos, sin = jnp.cos(ang), jnp.sin(ang)
    pad = HEAD_DIM - ROT_DIM
    ones = jnp.ones((seq, pad), F32)
    zeros = jnp.zeros((seq, pad), F32)
    zh = jnp.zeros((seq, half), F32)
    c_head = jnp.concatenate([cos, cos, ones], axis=1)
    s1_head = jnp.concatenate([-sin, zh, zeros], axis=1)
    s2_head = jnp.concatenate([zh, sin, zeros], axis=1)
    rep = LANES // HEAD_DIM
    return (jnp.tile(c_head, (1, rep)), jnp.tile(s1_head, (1, rep)), jnp.tile(s2_head, (1, rep)))


def _dispatch_lists(ridx, rw, chunk):
    n = ridx.shape[1]
    n_chunks = n // chunk
    e_flat = ridx.T.reshape(n_chunks, chunk * TOP_K)
    w_flat = rw.T.reshape(n_chunks, chunk * TOP_K)
    n_asg = chunk * TOP_K
    nb = _moe_blocks(chunk)
    pos = lax.broadcasted_iota(jnp.int32, e_flat.shape, 1)
    key_sorted, w_sorted = lax.sort((e_flat * n_asg + pos, w_flat), dimension=1, is_stable=False, num_keys=1)
    tok_sorted = (key_sorted % n_asg) // TOP_K * ROW_VREGS
    sorted_pair = jnp.stack([tok_sorted.astype(F32), w_sorted], axis=-1)
    experts = jnp.arange(N_EXPERTS, dtype=jnp.int32)
    counts = jnp.sum((e_flat[:, :, None] == experts).astype(jnp.int32), axis=1)
    start = jnp.cumsum(counts, axis=1) - counts
    n_blk = (counts + MOE_BM - 1) // MOE_BM
    b_end = jnp.cumsum(n_blk, axis=1)
    b_start = b_end - n_blk
    n_active = b_end[:, -1]

    jb = jnp.arange(nb, dtype=jnp.int32)
    eb = jnp.minimum(jnp.sum((jb[None, :, None] >= b_end[:, None, :]).astype(jnp.int32), axis=2), N_EXPERTS - 1)
    take = lambda a: jnp.take_along_axis(a, eb, axis=1)
    row = (jb[None, :] - take(b_start))[:, :, None] * MOE_BM + jnp.arange(MOE_BM, dtype=jnp.int32)
    valid = (jb[None, :, None] < n_active[:, None, None]) & (row < take(counts)[:, :, None])
    src = jnp.clip(take(start)[:, :, None] + row, 0, n_asg - 1).reshape(n_chunks, -1)
    pair = jnp.take_along_axis(sorted_pair, src[:, :, None], axis=1).reshape(valid.shape + (2,))
    tok = jnp.where(valid, pair[..., 0].astype(jnp.int32), 0)
    wgt = jnp.where(valid, pair[..., 1], 0.0)

    def padded(a, before, after):
        pad = lambda k: jnp.zeros((n_chunks, k, MOE_BM), a.dtype)
        return jnp.concatenate([pad(before), a, pad(after)], axis=1)
    tok_blocks = MOE_LAG_SCATTER + nb + MOE_EXTRA_STEPS
    align = SMEM_BLOCK_1D // MOE_BM
    tail = MOE_EXTRA_STEPS + (-tok_blocks) % align
    chunk_tok = padded(tok, MOE_LAG_SCATTER, tail).reshape(-1)
    chunk_w = padded(wgt, MOE_LAG_EXPERT, MOE_EXTRA_STEPS - MOE_LAG_EXPERT)
    return chunk_tok, chunk_w, eb.reshape(-1), n_active


def kernel(x, mem, w_in, w_pool, pool_scale, w_mem_kv, w_out, ln1_g, ln1_b, w_router, router_bias,
           w_gate, w_up, w_down, ws_gate, ws_up, ws_down, ln2_g, ln2_b):
    batch, seq, d = x.shape
    depth = w_in.shape[0]
    assert d == D_MODEL and seq % (16 * QBLK) == 0 and seq % TM_PROJ == 0
    assert (seq // QBLK - 1) % DSA_UNROLL1 == 0 and (seq // QBLK) % max(DSA_UNROLL16, MEM_UNROLL) == 0
    n = batch * seq
    chunk = min(MOE_CHUNK, n)
    assert n % chunk == 0 and chunk % TM_PROJ == 0
    alpha = float((2 * depth) ** 0.25)

    rope_c, rope_s1, rope_s2 = _rope_tables(seq)
    wg_bf, wu_bf, wd_bf = w_gate.astype(BF16), w_up.astype(BF16), w_down.astype(BF16)
    x2d = x.reshape(n, d)
    mem2d = mem.reshape(batch * mem.shape[1], d)

    for l in range(depth):
        w_in_bf = w_in[l].astype(BF16)
        wkv_bf = w_mem_kv[l].astype(BF16)
        wo_bf = w_out[l].astype(BF16)
        wpool_bd = jax.scipy.linalg.block_diag(*[w_pool[l, g] for g in range(len(POOL_WINDOWS))]).astype(BF16)
        pscale = pool_scale[l].reshape(1, POOL_WIDTH)
        wr_t = w_router[l].T
        wrh = wr_t.astype(BF16)
        wrl = (wr_t - wrh.astype(F32)).astype(BF16)
        rbias = router_bias[l].reshape(N_EXPERTS, 1)

        u, q, k, v, qm = _in_projection(x2d, w_in_bf, rope_c, rope_s1, rope_s2, seq)
        ydsa = _dilated_attention(q, k, v, batch, seq)
        ypm = _pool_and_memory(u, qm, mem2d, wkv_bf, wpool_bd, pscale, batch, seq)
        x1, x1g, ridx, rw = _out_projection(
            x2d, ypm, ydsa, wo_bf, ln1_g[l].reshape(1, d), ln1_b[l].reshape(1, d), wrh, wrl, rbias, alpha)
        chunk_tok, chunk_w, block_expert, n_active = _dispatch_lists(ridx, rw, chunk)
        acc = _routed_experts(x1g, chunk_tok, chunk_w, block_expert, n_active, wg_bf, wu_bf, wd_bf, l, chunk)
        x2d = _shared_and_norm(
            x1, acc, ws_gate[l].astype(BF16), ws_up[l].astype(BF16), ws_down[l].astype(BF16),
            ln2_g[l].reshape(1, d), ln2_b[l].reshape(1, d), alpha, chunk)
    return x2d.reshape(batch, seq, d)
```

```python
import functools

import jax
import jax.numpy as jnp
from jax import lax
from jax.experimental import pallas as pl
from jax.experimental.pallas import tpu as pltpu

F32 = jnp.float32
BF16 = jnp.bfloat16

D_MODEL = 1024
HEAD_DIM = 64
POOL_WINDOWS = (2, 4, 8, 16)
POOL_WIDTH = 256
POOL_GROUP = 64
DSA_WIDTH = 512
DSA_LOOKBACK = 128
DSA_DILATIONS = (1, 4, 16)
MEM_WIDTH = 256
IN_WIDTH = 2048
ROPE_THETA = 500000.0
ROT_DIM = 16
N_EXPERTS = 64
TOP_K = 8
N_GROUPS = 8
TOPK_GROUPS = 4
D_EXPERT = 256
ROUTED_SCALE = 2.5
LN_EPS = 1e-5

LANES = 128
SUBLANES = 8
SMEM_BLOCK_1D = 1024
ROW_VREGS = D_MODEL // LANES

NEG_BIG = -1e30

TM_PROJ = 512
QBLK = 128
DSA_UNROLL1 = 15
DSA_RES4 = 4
DSA_UNROLL16 = 16
MEM_UNROLL = 16
MOE_CHUNK = 4096
MOE_BM = 256
MOE_UNROLL = 8
MOE_SPLIT = 1
MOE_LAG_EXPERT = 1
MOE_LAG_SCATTER = 2
MOE_EXTRA_STEPS = MOE_LAG_SCATTER + 1
VMEM_LIMIT = 56 * 1024 * 1024


def _cparams(sem):
    return pltpu.CompilerParams(dimension_semantics=sem, vmem_limit_bytes=VMEM_LIMIT)


def _inproj_kernel(x_ref, w_ref, c_ref, s1_ref, s2_ref, u_ref, q_ref, k_ref, v_ref, qm_ref):
    xb = x_ref[...].astype(BF16)
    h = jnp.dot(xb, w_ref[...], preferred_element_type=F32)
    c = c_ref[...]
    s1 = s1_ref[...]
    s2 = s2_ref[...]

    def rot(a):
        return a * c + pltpu.roll(a, LANES - ROT_DIM // 2, 1) * s1 + pltpu.roll(a, ROT_DIM // 2, 1) * s2

    scale = HEAD_DIM ** -0.5
    u_ref[...] = h[:, :POOL_WIDTH]
    q0 = POOL_WIDTH
    k0 = q0 + DSA_WIDTH
    v0 = k0 + DSA_WIDTH
    m0 = v0 + DSA_WIDTH
    for j in range(DSA_WIDTH // LANES):
        q_ref[:, j * LANES:(j + 1) * LANES] = rot(h[:, q0 + j * LANES:q0 + (j + 1) * LANES]) * scale
        k_ref[:, j * LANES:(j + 1) * LANES] = rot(h[:, k0 + j * LANES:k0 + (j + 1) * LANES])
    v_ref[...] = h[:, v0:m0]
    qm_ref[...] = h[:, m0:] * scale


def _in_projection(x2d, w_in_bf, rope_c, rope_s1, rope_s2, seq):
    n = x2d.shape[0]
    tm = TM_PROJ
    per_seq = seq // tm
    row = lambda i: (i, 0)
    tab = lambda i: (i % per_seq, 0)
    return pl.pallas_call(
        _inproj_kernel,
        grid=(n // tm,),
        in_specs=[
            pl.BlockSpec((tm, D_MODEL), row),
            pl.BlockSpec((D_MODEL, IN_WIDTH), lambda i: (0, 0)),
            pl.BlockSpec((tm, LANES), tab),
            pl.BlockSpec((tm, LANES), tab),
            pl.BlockSpec((tm, LANES), tab),
        ],
        out_specs=[
            pl.BlockSpec((tm, POOL_WIDTH), row),
            pl.BlockSpec((tm, DSA_WIDTH), row),
            pl.BlockSpec((tm, DSA_WIDTH), row),
            pl.BlockSpec((tm, DSA_WIDTH), row),
            pl.BlockSpec((tm, MEM_WIDTH), row),
        ],
        out_shape=[
            jax.ShapeDtypeStruct((n, POOL_WIDTH), F32),
            jax.ShapeDtypeStruct((n, DSA_WIDTH), F32),
            jax.ShapeDtypeStruct((n, DSA_WIDTH), F32),
            jax.ShapeDtypeStruct((n, DSA_WIDTH), F32),
            jax.ShapeDtypeStruct((n, MEM_WIDTH), F32),
        ],
        compiler_params=_cparams(("parallel",)),
        name="in_projection",
    )(x2d, w_in_bf, rope_c, rope_s1, rope_s2)


def _two_head_attention(q2, kk, vv, bias):
    s = lax.dot_general(q2, kk, (((1,), (1,)), ((), ())), preferred_element_type=F32)
    if bias is not None:
        s = s + bias
    m = jnp.max(s, axis=-1, keepdims=True)
    p = jnp.exp(s - m)
    l = jnp.sum(p, axis=-1, keepdims=True)
    o2 = jnp.dot(p.astype(BF16), vv, preferred_element_type=F32) / l
    lse2 = m + jnp.log(l)
    first = lax.broadcasted_iota(jnp.int32, (QBLK, LANES), 1) < HEAD_DIM
    o = jnp.where(first, o2[:QBLK], o2[QBLK:])
    lse = jnp.where(first, lse2[:QBLK], lse2[QBLK:])
    return o, lse


def _band_bias(nk, dbase):
    qi = lax.broadcasted_iota(jnp.int32, (2 * QBLK, nk), 0) & (QBLK - 1)
    kj = lax.broadcasted_iota(jnp.int32, (2 * QBLK, nk), 1)
    diff = qi - kj + dbase
    ok = lax.bitcast_convert_type(diff, jnp.uint32) <= jnp.uint32(DSA_LOOKBACK)
    return jnp.where(ok, 0.0, NEG_BIG).astype(F32)


def _dsa_kernel(q_ref, k_ref, v_ref, o_ref,
                qa1, qb1, k1, v1, qa4, qb4, k4, v4, qa16, qb16, k16, v16,
                o1, l1, o4, l4, o16, l16):
    seq = q_ref.shape[0]

    def stage(dil, qa, qb, kd, vd):
        sub = seq // dil
        for r in range(dil):
            if dil == 1:
                qv, kv, vv = q_ref[...], k_ref[...], v_ref[...]
            else:
                qv = q_ref[pl.ds(r, sub, stride=dil), :]
                kv = k_ref[pl.ds(r, sub, stride=dil), :]
                vv = v_ref[pl.ds(r, sub, stride=dil), :]
            fm = lax.broadcasted_iota(jnp.int32, (sub, LANES), 1) < HEAD_DIM
            qa[r * sub:(r + 1) * sub, :] = jnp.where(fm, qv, 0.0).astype(BF16)
            qb[r * sub:(r + 1) * sub, :] = jnp.where(fm, 0.0, qv).astype(BF16)
            kd[r * sub:(r + 1) * sub, :] = kv.astype(BF16)
            vd[r * sub:(r + 1) * sub, :] = vv.astype(BF16)

    stage(1, qa1, qb1, k1, v1)
    stage(4, qa4, qb4, k4, v4)
    stage(16, qa16, qb16, k16, v16)

    n_blocks = seq // QBLK
    causal = _band_bias(QBLK, 0)
    banded = _band_bias(2 * QBLK, QBLK)

    def block(bufs, qrow, with_prev):
        qa, qb, kd, vd, od, ld = bufs
        qrow = pl.multiple_of(qrow, QBLK)
        q2 = jnp.concatenate([qa[pl.ds(qrow, QBLK), :], qb[pl.ds(qrow, QBLK), :]], axis=0)
        if with_prev:
            krow, nk, bias = pl.multiple_of(qrow - QBLK, QBLK), 2 * QBLK, banded
        else:
            krow, nk, bias = qrow, QBLK, causal
        o, lse = _two_head_attention(q2, kd[pl.ds(krow, nk), :], vd[pl.ds(krow, nk), :], bias)
        od[pl.ds(qrow, QBLK), :] = o
        ld[pl.ds(qrow, QBLK), :] = lse

    bufs1 = (qa1, qb1, k1, v1, o1, l1)
    block(bufs1, 0, False)

    def body1(it, carry):
        for u in range(DSA_UNROLL1):
            block(bufs1, (1 + it * DSA_UNROLL1 + u) * QBLK, True)
        return carry
    lax.fori_loop(0, (n_blocks - 1) // DSA_UNROLL1, body1, 0)

    bufs4 = (qa4, qb4, k4, v4, o4, l4)
    per4 = n_blocks // 4

    def body4(it, carry):
        for rr in range(DSA_RES4):
            base = (it * DSA_RES4 + rr) * per4 * QBLK
            block(bufs4, base, False)
            for u in range(1, per4):
                block(bufs4, base + u * QBLK, True)
        return carry
    lax.fori_loop(0, 4 // DSA_RES4, body4, 0)

    bufs16 = (qa16, qb16, k16, v16, o16, l16)

    def body16(it, carry):
        for u in range(DSA_UNROLL16):
            block(bufs16, (it * DSA_UNROLL16 + u) * QBLK, False)
        return carry
    lax.fori_loop(0, n_blocks // DSA_UNROLL16, body16, 0)

    sub16 = seq // 16
    sub4 = seq // 4
    for r in range(16):
        nat = pl.ds(r, sub16, stride=16)
        res4 = pl.ds((r % 4) * sub4 + r // 4, sub16, stride=4)
        res16 = pl.ds(r * sub16, sub16)
        a_l, b_l, c_l = l1[nat, :], l4[res4, :], l16[res16, :]
        m = jnp.maximum(jnp.maximum(a_l, b_l), c_l)
        ea = jnp.exp(a_l - m)
        eb = jnp.exp(b_l - m)
        ec = jnp.exp(c_l - m)
        out = (ea * o1[nat, :] + eb * o4[res4, :] + ec * o16[res16, :]) / (ea + eb + ec)
        o_ref[nat, :] = out


def _dilated_attention(q, k, v, batch, seq):
    n = q.shape[0]
    pairs = DSA_WIDTH // LANES
    blk = pl.BlockSpec((seq, LANES), lambda b, j: (b, j))
    bf = lambda: pltpu.VMEM((seq, LANES), BF16)
    ff = lambda: pltpu.VMEM((seq, LANES), F32)
    return pl.pallas_call(
        _dsa_kernel,
        grid=(batch, pairs),
        in_specs=[blk, blk, blk],
        out_specs=blk,
        out_shape=jax.ShapeDtypeStruct((n, DSA_WIDTH), F32),
        scratch_shapes=[bf() for _ in range(12)] + [ff() for _ in range(6)],
        compiler_params=_cparams(("parallel", "parallel")),
        name="dilated_attention",
    )(q, k, v)


def _poolmem_kernel(u_ref, qm_ref, mem_ref, wkv_ref, wpool_ref, pscale_ref, y_ref, kv_s, qa_s, qb_s):
    seq = u_ref.shape[0]
    u = u_ref[...]
    rows = lax.broadcasted_iota(jnp.int32, (seq, POOL_WIDTH), 0)
    grp = lax.broadcasted_iota(jnp.int32, (seq, POOL_WIDTH), 1) // POOL_GROUP

    def shifted(a, kk):
        return jnp.where(rows >= kk, pltpu.roll(a, kk, 0), 0.0)

    s2 = u + shifted(u, 1)
    s4 = s2 + shifted(s2, 2)
    s8 = s4 + shifted(s4, 4)
    s16 = s8 + shifted(s8, 8)
    wsum = jnp.where(grp == 0, s2, jnp.where(grp == 1, s4, jnp.where(grp == 2, s8, s16)))
    win = jnp.where(grp == 0, 2, jnp.where(grp == 1, 4, jnp.where(grp == 2, 8, 16)))
    cnt = jnp.minimum(rows + 1, win).astype(F32)
    pooled = wsum / cnt - u
    mixed = jnp.dot(pooled.astype(BF16), wpool_ref[...], preferred_element_type=F32) * pscale_ref[...]
    y_ref[:, :POOL_WIDTH] = mixed

    kv_s[...] = jnp.dot(mem_ref[...].astype(BF16), wkv_ref[...], preferred_element_type=F32).astype(BF16)
    first = lax.broadcasted_iota(jnp.int32, (seq, LANES), 1) < HEAD_DIM
    for j in range(MEM_WIDTH // LANES):
        qv = qm_ref[:, j * LANES:(j + 1) * LANES]
        qa_s[...] = jnp.where(first, qv, 0.0).astype(BF16)
        qb_s[...] = jnp.where(first, 0.0, qv).astype(BF16)
        kk = kv_s[:, j * LANES:(j + 1) * LANES]
        vv = kv_s[:, MEM_WIDTH + j * LANES:MEM_WIDTH + (j + 1) * LANES]

        def body(it, carry):
            for u in range(MEM_UNROLL):
                qrow = pl.multiple_of((it * MEM_UNROLL + u) * QBLK, QBLK)
                q2 = jnp.concatenate([qa_s[pl.ds(qrow, QBLK), :], qb_s[pl.ds(qrow, QBLK), :]], axis=0)
                o, _ = _two_head_attention(q2, kk, vv, None)
                y_ref[pl.ds(qrow, QBLK), pl.ds(POOL_WIDTH + j * LANES, LANES)] = o
            return carry
        lax.fori_loop(0, seq // (QBLK * MEM_UNROLL), body, 0)


def _pool_and_memory(u, qm, mem2d, wkv_bf, wpool_bd, pscale, batch, seq):
    n = u.shape[0]
    mem_len = mem2d.shape[0] // batch
    row = lambda b: (b, 0)
    fix = lambda b: (0, 0)
    return pl.pallas_call(
        _poolmem_kernel,
        grid=(batch,),
        in_specs=[
            pl.BlockSpec((seq, POOL_WIDTH), row),
            pl.BlockSpec((seq, MEM_WIDTH), row),
            pl.BlockSpec((mem_len, D_MODEL), row),
            pl.BlockSpec((D_MODEL, 2 * MEM_WIDTH), fix),
            pl.BlockSpec((POOL_WIDTH, POOL_WIDTH), fix),
            pl.BlockSpec((1, POOL_WIDTH), fix),
        ],
        out_specs=pl.BlockSpec((seq, POOL_WIDTH + MEM_WIDTH), row),
        out_shape=jax.ShapeDtypeStruct((n, POOL_WIDTH + MEM_WIDTH), F32),
        scratch_shapes=[
            pltpu.VMEM((mem_len, 2 * MEM_WIDTH), BF16),
            pltpu.VMEM((seq, LANES), BF16),
            pltpu.VMEM((seq, LANES), BF16),
        ],
        compiler_params=_cparams(("parallel",)),
        name="pool_memory",
    )(u, qm, mem2d, wkv_bf, wpool_bd, pscale)


def _layer_norm(z, g, b):
    mu = jnp.mean(z, axis=-1, keepdims=True)
    zc = z - mu
    var = jnp.mean(zc * zc, axis=-1, keepdims=True)
    return zc * lax.rsqrt(var + LN_EPS) * g + b


def _store_token_rows(dst_ref, val, first=0):
    tm = val.shape[0]
    for s in range(ROW_VREGS):
        dst_ref[pl.ds(first * ROW_VREGS + s, tm, stride=ROW_VREGS), :] = val[:, s * LANES:(s + 1) * LANES]


def _load_token_rows(src_ref, tm, first=0):
    return jnp.concatenate(
        [src_ref[pl.ds(first * ROW_VREGS + s, tm, stride=ROW_VREGS), :] for s in range(ROW_VREGS)], axis=1)


def _route(x1, wrh, wrl, rbias, ridx_ref, rw_ref):
    tm = x1.shape[0]
    xh = x1.astype(BF16)
    xl = (x1 - xh.astype(F32)).astype(BF16)
    nt = (((1,), (1,)), ((), ()))
    logits = (lax.dot_general(wrh, xh, nt, preferred_element_type=F32)
              + lax.dot_general(wrh, xl, nt, preferred_element_type=F32)
              + lax.dot_general(wrl, xh, nt, preferred_element_type=F32))
    scores = jax.nn.sigmoid(logits)
    biased = scores + rbias

    per = N_EXPERTS // N_GROUPS
    io8 = lax.broadcasted_iota(jnp.int32, (per, tm), 0)
    gscores = []
    for g in range(N_GROUPS):
        vg = biased[g * per:(g + 1) * per]
        m1 = jnp.max(vg, axis=0, keepdims=True)
        i1 = jnp.min(jnp.where(vg == m1, io8, per), axis=0, keepdims=True)
        m2 = jnp.max(jnp.where(io8 == i1, -jnp.inf, vg), axis=0, keepdims=True)
        gscores.append(m1 + m2)
    gs = jnp.concatenate(gscores, axis=0)
    iog = lax.broadcasted_iota(jnp.int32, (N_GROUPS, tm), 0)
    grank = jnp.zeros((N_GROUPS, tm), F32)
    for g in range(N_GROUPS):
        sg = gs[g:g + 1]
        tie = jnp.where(iog > g, 1.0, 0.0)
        grank = grank + jnp.where(sg > gs, 1.0, jnp.where(sg == gs, tie, 0.0))
    gsel = grank < TOPK_GROUPS
    masked = jnp.concatenate(
        [jnp.where(gsel[g:g + 1], biased[g * per:(g + 1) * per], -jnp.inf) for g in range(N_GROUPS)], axis=0)

    ioe = lax.broadcasted_iota(jnp.int32, (N_EXPERTS, tm), 0)
    rank = jnp.zeros((N_EXPERTS, tm), F32)
    for e in range(N_EXPERTS):
        me = masked[e:e + 1]
        tie = jnp.where(ioe > e, 1.0, 0.0)
        rank = rank + jnp.where(me > masked, 1.0, jnp.where(me == masked, tie, 0.0))

    idx_rows, w_rows = [], []
    for kk in range(TOP_K):
        hit = rank == float(kk)
        idx_rows.append(jnp.sum(jnp.where(hit, ioe, 0), axis=0, keepdims=True))
        w_rows.append(jnp.sum(jnp.where(hit, scores, 0.0), axis=0, keepdims=True))
    top_w = jnp.concatenate(w_rows, axis=0)
    denom = jnp.sum(top_w, axis=0, keepdims=True)
    ridx_ref[...] = jnp.concatenate(idx_rows, axis=0)
    rw_ref[...] = top_w / denom * ROUTED_SCALE


def _outproj_kernel(alpha, x_ref, ypm_ref, ydsa_ref, wo_ref, g_ref, b_ref, wrh_ref, wrl_ref, rb_ref,
                    x1_ref, x1g_ref, ridx_ref, rw_ref):
    y = jnp.concatenate([ypm_ref[:, :POOL_WIDTH], ydsa_ref[...], ypm_ref[:, POOL_WIDTH:]], axis=1)
    mix = jnp.dot(y.astype(BF16), wo_ref[...], preferred_element_type=F32)
    x1 = _layer_norm(alpha * x_ref[...] + mix, g_ref[...], b_ref[...])
    x1_ref[...] = x1
    _store_token_rows(x1g_ref, x1)
    _route(x1, wrh_ref[...], wrl_ref[...], rb_ref[...], ridx_ref, rw_ref)


def _out_projection(x2d, ypm, ydsa, wo_bf, g, b, wrh, wrl, rbias, alpha):
    n = x2d.shape[0]
    tm = TM_PROJ
    row = lambda i: (i, 0)
    fix = lambda i: (0, 0)
    col = lambda i: (0, i)
    return pl.pallas_call(
        functools.partial(_outproj_kernel, alpha),
        grid=(n // tm,),
        in_specs=[
            pl.BlockSpec((tm, D_MODEL), row),
            pl.BlockSpec((tm, POOL_WIDTH + MEM_WIDTH), row),
            pl.BlockSpec((tm, DSA_WIDTH), row),
            pl.BlockSpec((D_MODEL, D_MODEL), fix),
            pl.BlockSpec((1, D_MODEL), fix),
            pl.BlockSpec((1, D_MODEL), fix),
            pl.BlockSpec((N_EXPERTS, D_MODEL), fix),
            pl.BlockSpec((N_EXPERTS, D_MODEL), fix),
            pl.BlockSpec((N_EXPERTS, 1), fix),
        ],
        out_specs=[
            pl.BlockSpec((tm, D_MODEL), row),
            pl.BlockSpec((tm * ROW_VREGS, LANES), row),
            pl.BlockSpec((TOP_K, tm), col),
            pl.BlockSpec((TOP_K, tm), col),
        ],
        out_shape=[
            jax.ShapeDtypeStruct((n, D_MODEL), F32),
            jax.ShapeDtypeStruct((n * ROW_VREGS, LANES), F32),
            jax.ShapeDtypeStruct((TOP_K, n), jnp.int32),
            jax.ShapeDtypeStruct((TOP_K, n), F32),
        ],
        compiler_params=_cparams(("parallel",)),
        name="out_projection_router",
    )(x2d, ypm, ydsa, wo_bf, g, b, wrh, wrl, rbias)


def _moe_blocks(chunk):
    return chunk * TOP_K // MOE_BM + N_EXPERTS


def _as_column(w_row):
    bm = w_row.shape[1]
    eye = lax.broadcasted_iota(jnp.int32, (bm, bm), 0) == lax.broadcasted_iota(jnp.int32, (bm, bm), 1)
    return jnp.sum(jnp.where(eye, w_row, 0.0), axis=1, keepdims=True)


def _moe_kernel(eb_ref, nact_ref, tok_ref, wrow_ref, xg_ref, wg_ref, wu_ref, wd_ref, acc_ref,
                rows0, rows1, y0, y1):
    c = pl.program_id(0)
    j = pl.program_id(1)

    @pl.when(j == 0)
    def _():
        acc_ref[...] = jnp.zeros(acc_ref.shape, F32)

    @pl.when(jnp.logical_and(c == 0, j == 0))
    def _():
        for buf in (rows0, rows1, y0, y1):
            buf[...] = jnp.zeros(buf.shape, F32)

    def step(rows_w, rows_r, y_w, y_r):
        scatter0 = j * MOE_BM
        gather0 = scatter0 + MOE_LAG_SCATTER * MOE_BM
        for i in range(MOE_BM):
            src = pl.multiple_of(tok_ref[gather0 + i], ROW_VREGS)
            rows_w[i * ROW_VREGS:(i + 1) * ROW_VREGS, :] = xg_ref[pl.ds(src, ROW_VREGS), :]

        for g0 in range(0, MOE_BM, MOE_UNROLL):
            upd = []
            for i in range(g0, g0 + MOE_UNROLL):
                dst = pl.multiple_of(tok_ref[scatter0 + i], ROW_VREGS)
                upd.append((dst, acc_ref[pl.ds(dst, ROW_VREGS), :] + y_r[i * ROW_VREGS:(i + 1) * ROW_VREGS, :]))
            for dst, val in reversed(upd):
                acc_ref[pl.ds(dst, ROW_VREGS), :] = val

        wcol = _as_column(wrow_ref[pl.ds(j, 1), :])
        sub = MOE_BM // MOE_SPLIT
        for h in range(MOE_SPLIT):
            xb = _load_token_rows(rows_r, sub, h * sub).astype(BF16)
            gg = jnp.dot(xb, wg_ref[...], preferred_element_type=F32)
            uu = jnp.dot(xb, wu_ref[...], preferred_element_type=F32)
            hh = gg * jax.nn.sigmoid(gg) * uu * wcol[h * sub:(h + 1) * sub]
            yy = jnp.dot(hh.astype(BF16), wd_ref[...], preferred_element_type=F32)
            _store_token_rows(y_w, yy, h * sub)

    active = j < nact_ref[c] + MOE_EXTRA_STEPS

    @pl.when(jnp.logical_and(active, j % 2 == 0))
    def _():
        step(rows0, rows1, y1, y0)

    @pl.when(jnp.logical_and(active, j % 2 == 1))
    def _():
        step(rows1, rows0, y0, y1)


def _routed_experts(x1g, chunk_tok, chunk_w, block_expert, n_active, w_gate, w_up, w_down, layer, chunk):
    n_tok = x1g.shape[0] // ROW_VREGS
    n_chunks = n_tok // chunk
    nb = _moe_blocks(chunk)
    steps = nb + MOE_EXTRA_STEPS
    acc_rows = chunk * ROW_VREGS

    def expert_w(c, j, eb, na):
        return (layer, eb[c * nb + jnp.clip(j - MOE_LAG_EXPERT, 0, nb - 1)], 0, 0)

    per_chunk = lambda c, j, eb, na: (c, 0, 0)
    rows = lambda: pltpu.VMEM((MOE_BM * ROW_VREGS, LANES), F32)
    grid_spec = pltpu.PrefetchScalarGridSpec(
        num_scalar_prefetch=2,
        grid=(n_chunks, steps),
        in_specs=[
            pl.BlockSpec((chunk_tok.shape[0] // n_chunks,), lambda c, j, eb, na: (c,), memory_space=pltpu.SMEM),
            pl.BlockSpec((None,) + chunk_w.shape[1:], per_chunk),
            pl.BlockSpec((chunk * ROW_VREGS, LANES), lambda c, j, eb, na: (c, 0), pipeline_mode=pl.Buffered(1)),
            pl.BlockSpec((None, None, D_MODEL, D_EXPERT), expert_w),
            pl.BlockSpec((None, None, D_MODEL, D_EXPERT), expert_w),
            pl.BlockSpec((None, None, D_EXPERT, D_MODEL), expert_w),
        ],
        out_specs=pl.BlockSpec((None, acc_rows, LANES), per_chunk, pipeline_mode=pl.Buffered(1)),
        scratch_shapes=[rows(), rows(), rows(), rows()],
    )
    return pl.pallas_call(
        _moe_kernel,
        grid_spec=grid_spec,
        out_shape=jax.ShapeDtypeStruct((n_chunks, acc_rows, LANES), F32),
        compiler_params=_cparams(("arbitrary", "arbitrary")),
        name="routed_experts",
    )(block_expert, n_active, chunk_tok, chunk_w, x1g, w_gate, w_up, w_down)


def _shared_kernel(alpha, x_ref, acc_ref, wg_ref, wu_ref, wd_ref, g_ref, b_ref, o_ref):
    x1 = x_ref[...]
    tm = x1.shape[0]
    xb = x1.astype(BF16)
    gg = jnp.dot(xb, wg_ref[...], preferred_element_type=F32)
    uu = jnp.dot(xb, wu_ref[...], preferred_element_type=F32)
    hh = (gg * jax.nn.sigmoid(gg) * uu).astype(BF16)
    shared = jnp.dot(hh, wd_ref[...], preferred_element_type=F32)
    routed = _load_token_rows(acc_ref, tm)
    o_ref[...] = _layer_norm(alpha * x1 + (routed + shared), g_ref[...], b_ref[...])


def _shared_and_norm(x1, acc, wsg, wsu, wsd, g, b, alpha, chunk):
    n = x1.shape[0]
    tm = TM_PROJ
    per_chunk = chunk // tm
    row = lambda i: (i, 0)
    fix = lambda i: (0, 0)
    return pl.pallas_call(
        functools.partial(_shared_kernel, alpha),
        grid=(n // tm,),
        in_specs=[
            pl.BlockSpec((tm, D_MODEL), row),
            pl.BlockSpec((None, tm * ROW_VREGS, LANES), lambda i: (i // per_chunk, i % per_chunk, 0)),
            pl.BlockSpec((D_MODEL, D_EXPERT), fix),
            pl.BlockSpec((D_MODEL, D_EXPERT), fix),
            pl.BlockSpec((D_EXPERT, D_MODEL), fix),
            pl.BlockSpec((1, D_MODEL), fix),
            pl.BlockSpec((1, D_MODEL), fix),
        ],
        out_specs=pl.BlockSpec((tm, D_MODEL), row),
        out_shape=jax.ShapeDtypeStruct((n, D_MODEL), F32),
        compiler_params=_cparams(("parallel",)),
        name="shared_expert_norm",
    )(x1, acc, wsg, wsu, wsd, g, b)


def _rope_tables(seq):
    half = ROT_DIM // 2
    pos = jnp.arange(seq, dtype=F32)
    inv = ROPE_THETA ** (-jnp.arange(0, ROT_DIM, 2, dtype=F32) / ROT_DIM)
    ang = pos[:, None] * inv[None, :]
    cos, sin = jnp.cos(ang), jnp.sin(ang)
    pad = HEAD_DIM - ROT_DIM
    ones = jnp.ones((seq, pad), F32)
    zeros = jnp.zeros((seq, pad), F32)
    zh = jnp.zeros((seq, half), F32)
    c_head = jnp.concatenate([cos, cos, ones], axis=1)
    s1_head = jnp.concatenate([-sin, zh, zeros], axis=1)
    s2_head = jnp.concatenate([zh, sin, zeros], axis=1)
    rep = LANES // HEAD_DIM
    return (jnp.tile(c_head, (1, rep)), jnp.tile(s1_head, (1, rep)), jnp.tile(s2_head, (1, rep)))


def _dispatch_lists(ridx, rw, chunk):
    n = ridx.shape[1]
    n_chunks = n // chunk
    e_flat = ridx.T.reshape(n_chunks, chunk * TOP_K)
    w_flat = rw.T.reshape(n_chunks, chunk * TOP_K)
    n_asg = chunk * TOP_K
    nb = _moe_blocks(chunk)
    pos = lax.broadcasted_iota(jnp.int32, e_flat.shape, 1)
    key_sorted, w_sorted = lax.sort((e_flat * n_asg + pos, w_flat), dimension=1, is_stable=False, num_keys=1)
    tok_sorted = (key_sorted % n_asg) // TOP_K * ROW_VREGS
    experts = jnp.arange(N_EXPERTS, dtype=jnp.int32)
    counts = jnp.sum((e_flat[:, :, None] == experts).astype(jnp.int32), axis=1)
    start = jnp.cumsum(counts, axis=1) - counts
    n_blk = (counts + MOE_BM - 1) // MOE_BM
    b_end = jnp.cumsum(n_blk, axis=1)
    b_start = b_end - n_blk
    n_active = b_end[:, -1]

    jb = jnp.arange(nb, dtype=jnp.int32)
    eb = jnp.minimum(jnp.sum((jb[None, :, None] >= b_end[:, None, :]).astype(jnp.int32), axis=2), N_EXPERTS - 1)
    take = lambda a: jnp.take_along_axis(a, eb, axis=1)
    row = (jb[None, :] - take(b_start))[:, :, None] * MOE_BM + jnp.arange(MOE_BM, dtype=jnp.int32)
    valid = (jb[None, :, None] < n_active[:, None, None]) & (row < take(counts)[:, :, None])
    src = jnp.clip(take(start)[:, :, None] + row, 0, n_asg - 1).reshape(n_chunks, -1)
    tok = jnp.where(valid, jnp.take_along_axis(tok_sorted, src, axis=1).reshape(valid.shape), 0)
    wgt = jnp.where(valid, jnp.take_along_axis(w_sorted, src, axis=1).reshape(valid.shape), 0.0)

    def padded(a, before, after):
        pad = lambda k: jnp.zeros((n_chunks, k, MOE_BM), a.dtype)
        return jnp.concatenate([pad(before), a, pad(after)], axis=1)
    tok_blocks = MOE_LAG_SCATTER + nb + MOE_EXTRA_STEPS
    align = SMEM_BLOCK_1D // MOE_BM
    tail = MOE_EXTRA_STEPS + (-tok_blocks) % align
    chunk_tok = padded(tok, MOE_LAG_SCATTER, tail).reshape(-1)
    chunk_w = padded(wgt, MOE_LAG_EXPERT, MOE_EXTRA_STEPS - MOE_LAG_EXPERT)
    return chunk_tok, chunk_w, eb.reshape(-1), n_active


def kernel(x, mem, w_in, w_pool, pool_scale, w_mem_kv, w_out, ln1_g, ln1_b, w_router, router_bias,
           w_gate, w_up, w_down, ws_gate, ws_up, ws_down, ln2_g, ln2_b):
    batch, seq, d = x.shape
    depth = w_in.shape[0]
    assert d == D_MODEL and seq % (16 * QBLK) == 0 and seq % TM_PROJ == 0
    assert (seq // QBLK - 1) % DSA_UNROLL1 == 0 and (seq // QBLK) % max(DSA_UNROLL16, MEM_UNROLL) == 0
    n = batch * seq
    chunk = min(MOE_CHUNK, n)
    assert n % chunk == 0 and chunk % TM_PROJ == 0
    alpha = float((2 * depth) ** 0.25)

    rope_c, rope_s1, rope_s2 = _rope_tables(seq)
    wg_bf, wu_bf, wd_bf = w_gate.astype(BF16), w_up.astype(BF16), w_down.astype(BF16)
    x2d = x.reshape(n, d)
    mem2d = mem.reshape(batch * mem.shape[1], d)

    for l in range(depth):
        w_in_bf = w_in[l].astype(BF16)
        wkv_bf = w_mem_kv[l].astype(BF16)
        wo_bf = w_out[l].astype(BF16)
        wpool_bd = jax.scipy.linalg.block_diag(*[w_pool[l, g] for g in range(len(POOL_WINDOWS))]).astype(BF16)
        pscale = pool_scale[l].reshape(1, POOL_WIDTH)
        wr_t = w_router[l].T
        wrh = wr_t.astype(BF16)
        wrl = (wr_t - wrh.astype(F32)).astype(BF16)
        rbias = router_bias[l].reshape(N_EXPERTS, 1)

        u, q, k, v, qm = _in_projection(x2d, w_in_bf, rope_c, rope_s1, rope_s2, seq)
        ydsa = _dilated_attention(q, k, v, batch, seq)
        ypm = _pool_and_memory(u, qm, mem2d, wkv_bf, wpool_bd, pscale, batch, seq)
        x1, x1g, ridx, rw = _out_projection(
            x2d, ypm, ydsa, wo_bf, ln1_g[l].reshape(1, d), ln1_b[l].reshape(1, d), wrh, wrl, rbias, alpha)
        chunk_tok, chunk_w, block_expert, n_active = _dispatch_lists(ridx, rw, chunk)
        acc = _routed_experts(x1g, chunk_tok, chunk_w, block_expert, n_active, wg_bf, wu_bf, wd_bf, l, chunk)
        x2d = _shared_and_norm(
            x1, acc, ws_gate[l].astype(BF16), ws_up[l].astype(BF16), ws_down[l].astype(BF16),
            ln2_g[l].reshape(1, d), ln2_b[l].reshape(1, d), alpha, chunk)
    return x2d.reshape(batch, seq, d)
```

```python
import functools

import jax
import jax.numpy as jnp
from jax import lax
from jax.experimental import pallas as pl
from jax.experimental.pallas import tpu as pltpu

F32 = jnp.float32
BF16 = jnp.bfloat16

D_MODEL = 1024
HEAD_DIM = 64
POOL_WINDOWS = (2, 4, 8, 16)
POOL_WIDTH = 256
POOL_GROUP = 64
DSA_WIDTH = 512
DSA_LOOKBACK = 128
DSA_DILATIONS = (1, 4, 16)
MEM_WIDTH = 256
IN_WIDTH = 2048
ROPE_THETA = 500000.0
ROT_DIM = 16
N_EXPERTS = 64
TOP_K = 8
N_GROUPS = 8
TOPK_GROUPS = 4
D_EXPERT = 256
ROUTED_SCALE = 2.5
LN_EPS = 1e-5

LANES = 128
SUBLANES = 8
SMEM_BLOCK_1D = 1024
ROW_VREGS = D_MODEL // LANES

NEG_BIG = -1e30

TM_PROJ = 512
TM_WIDE = 1024
QBLK = 128
DSA_UNROLL1 = 15
DSA_RES4 = 4
DSA_UNROLL16 = 16
MEM_UNROLL = 16
MOE_CHUNK = 4096
MOE_BM = 256
MOE_UNROLL = 8
MOE_SPLIT = 1
MOE_LAG_EXPERT = 1
MOE_LAG_SCATTER = 2
MOE_EXTRA_STEPS = MOE_LAG_SCATTER + 1
VMEM_LIMIT = 56 * 1024 * 1024


def _cparams(sem):
    return pltpu.CompilerParams(dimension_semantics=sem, vmem_limit_bytes=VMEM_LIMIT)


def _inproj_kernel(x_ref, w_ref, c_ref, s1_ref, s2_ref, u_ref, q_ref, k_ref, v_ref, qm_ref):
    xb = x_ref[...].astype(BF16)
    h = jnp.dot(xb, w_ref[...], preferred_element_type=F32)
    c = c_ref[...]
    s1 = s1_ref[...]
    s2 = s2_ref[...]

    def rot(a):
        return a * c + pltpu.roll(a, LANES - ROT_DIM // 2, 1) * s1 + pltpu.roll(a, ROT_DIM // 2, 1) * s2

    scale = HEAD_DIM ** -0.5
    u_ref[...] = h[:, :POOL_WIDTH]
    q0 = POOL_WIDTH
    k0 = q0 + DSA_WIDTH
    v0 = k0 + DSA_WIDTH
    m0 = v0 + DSA_WIDTH
    for j in range(DSA_WIDTH // LANES):
        q_ref[:, j * LANES:(j + 1) * LANES] = rot(h[:, q0 + j * LANES:q0 + (j + 1) * LANES]) * scale
        k_ref[:, j * LANES:(j + 1) * LANES] = rot(h[:, k0 + j * LANES:k0 + (j + 1) * LANES])
    v_ref[...] = h[:, v0:m0]
    qm_ref[...] = h[:, m0:] * scale


def _in_projection(x2d, w_in_bf, rope_c, rope_s1, rope_s2, seq):
    n = x2d.shape[0]
    tm = TM_WIDE
    per_seq = seq // tm
    row = lambda i: (i, 0)
    tab = lambda i: (i % per_seq, 0)
    return pl.pallas_call(
        _inproj_kernel,
        grid=(n // tm,),
        in_specs=[
            pl.BlockSpec((tm, D_MODEL), row),
            pl.BlockSpec((D_MODEL, IN_WIDTH), lambda i: (0, 0)),
            pl.BlockSpec((tm, LANES), tab),
            pl.BlockSpec((tm, LANES), tab),
            pl.BlockSpec((tm, LANES), tab),
        ],
        out_specs=[
            pl.BlockSpec((tm, POOL_WIDTH), row),
            pl.BlockSpec((tm, DSA_WIDTH), row),
            pl.BlockSpec((tm, DSA_WIDTH), row),
            pl.BlockSpec((tm, DSA_WIDTH), row),
            pl.BlockSpec((tm, MEM_WIDTH), row),
        ],
        out_shape=[
            jax.ShapeDtypeStruct((n, POOL_WIDTH), F32),
            jax.ShapeDtypeStruct((n, DSA_WIDTH), F32),
            jax.ShapeDtypeStruct((n, DSA_WIDTH), F32),
            jax.ShapeDtypeStruct((n, DSA_WIDTH), F32),
            jax.ShapeDtypeStruct((n, MEM_WIDTH), F32),
        ],
        compiler_params=_cparams(("parallel",)),
        name="in_projection",
    )(x2d, w_in_bf, rope_c, rope_s1, rope_s2)


def _two_head_attention(q2, kk, vv, bias):
    s = lax.dot_general(q2, kk, (((1,), (1,)), ((), ())), preferred_element_type=F32)
    if bias is not None:
        s = s + bias
    m = jnp.max(s, axis=-1, keepdims=True)
    p = jnp.exp(s - m)
    l = jnp.sum(p, axis=-1, keepdims=True)
    o2 = jnp.dot(p.astype(BF16), vv, preferred_element_type=F32) / l
    lse2 = m + jnp.log(l)
    first = lax.broadcasted_iota(jnp.int32, (QBLK, LANES), 1) < HEAD_DIM
    o = jnp.where(first, o2[:QBLK], o2[QBLK:])
    lse = jnp.where(first, lse2[:QBLK], lse2[QBLK:])
    return o, lse


def _band_bias(nk, dbase):
    qi = lax.broadcasted_iota(jnp.int32, (2 * QBLK, nk), 0) & (QBLK - 1)
    kj = lax.broadcasted_iota(jnp.int32, (2 * QBLK, nk), 1)
    diff = qi - kj + dbase
    ok = lax.bitcast_convert_type(diff, jnp.uint32) <= jnp.uint32(DSA_LOOKBACK)
    return jnp.where(ok, 0.0, NEG_BIG).astype(F32)


def _dsa_kernel(q_ref, k_ref, v_ref, o_ref,
                qa1, qb1, k1, v1, qa4, qb4, k4, v4, qa16, qb16, k16, v16,
                o1, l1, o4, l4, o16, l16):
    seq = q_ref.shape[0]

    def stage(dil, qa, qb, kd, vd):
        sub = seq // dil
        for r in range(dil):
            if dil == 1:
                qv, kv, vv = q_ref[...], k_ref[...], v_ref[...]
            else:
                qv = q_ref[pl.ds(r, sub, stride=dil), :]
                kv = k_ref[pl.ds(r, sub, stride=dil), :]
                vv = v_ref[pl.ds(r, sub, stride=dil), :]
            fm = lax.broadcasted_iota(jnp.int32, (sub, LANES), 1) < HEAD_DIM
            qa[r * sub:(r + 1) * sub, :] = jnp.where(fm, qv, 0.0).astype(BF16)
            qb[r * sub:(r + 1) * sub, :] = jnp.where(fm, 0.0, qv).astype(BF16)
            kd[r * sub:(r + 1) * sub, :] = kv.astype(BF16)
            vd[r * sub:(r + 1) * sub, :] = vv.astype(BF16)

    stage(1, qa1, qb1, k1, v1)
    stage(4, qa4, qb4, k4, v4)
    stage(16, qa16, qb16, k16, v16)

    n_blocks = seq // QBLK
    causal = _band_bias(QBLK, 0)
    banded = _band_bias(2 * QBLK, QBLK)

    def block(bufs, qrow, with_prev):
        qa, qb, kd, vd, od, ld = bufs
        qrow = pl.multiple_of(qrow, QBLK)
        q2 = jnp.concatenate([qa[pl.ds(qrow, QBLK), :], qb[pl.ds(qrow, QBLK), :]], axis=0)
        if with_prev:
            krow, nk, bias = pl.multiple_of(qrow - QBLK, QBLK), 2 * QBLK, banded
        else:
            krow, nk, bias = qrow, QBLK, causal
        o, lse = _two_head_attention(q2, kd[pl.ds(krow, nk), :], vd[pl.ds(krow, nk), :], bias)
        od[pl.ds(qrow, QBLK), :] = o
        ld[pl.ds(qrow, QBLK), :] = lse

    bufs1 = (qa1, qb1, k1, v1, o1, l1)
    block(bufs1, 0, False)

    def body1(it, carry):
        for u in range(DSA_UNROLL1):
            block(bufs1, (1 + it * DSA_UNROLL1 + u) * QBLK, True)
        return carry
    lax.fori_loop(0, (n_blocks - 1) // DSA_UNROLL1, body1, 0)

    bufs4 = (qa4, qb4, k4, v4, o4, l4)
    per4 = n_blocks // 4

    def body4(it, carry):
        for rr in range(DSA_RES4):
            base = (it * DSA_RES4 + rr) * per4 * QBLK
            block(bufs4, base, False)
            for u in range(1, per4):
                block(bufs4, base + u * QBLK, True)
        return carry
    lax.fori_loop(0, 4 // DSA_RES4, body4, 0)

    bufs16 = (qa16, qb16, k16, v16, o16, l16)

    def body16(it, carry):
        for u in range(DSA_UNROLL16):
            block(bufs16, (it * DSA_UNROLL16 + u) * QBLK, False)
        return carry
    lax.fori_loop(0, n_blocks // DSA_UNROLL16, body16, 0)

    sub16 = seq // 16
    sub4 = seq // 4
    for r in range(16):
        nat = pl.ds(r, sub16, stride=16)
        res4 = pl.ds((r % 4) * sub4 + r // 4, sub16, stride=4)
        res16 = pl.ds(r * sub16, sub16)
        a_l, b_l, c_l = l1[nat, :], l4[res4, :], l16[res16, :]
        m = jnp.maximum(jnp.maximum(a_l, b_l), c_l)
        ea = jnp.exp(a_l - m)
        eb = jnp.exp(b_l - m)
        ec = jnp.exp(c_l - m)
        out = (ea * o1[nat, :] + eb * o4[res4, :] + ec * o16[res16, :]) / (ea + eb + ec)
        o_ref[nat, :] = out


def _dilated_attention(q, k, v, batch, seq):
    n = q.shape[0]
    pairs = DSA_WIDTH // LANES
    blk = pl.BlockSpec((seq, LANES), lambda b, j: (b, j))
    bf = lambda: pltpu.VMEM((seq, LANES), BF16)
    ff = lambda: pltpu.VMEM((seq, LANES), F32)
    return pl.pallas_call(
        _dsa_kernel,
        grid=(batch, pairs),
        in_specs=[blk, blk, blk],
        out_specs=blk,
        out_shape=jax.ShapeDtypeStruct((n, DSA_WIDTH), F32),
        scratch_shapes=[bf() for _ in range(12)] + [ff() for _ in range(6)],
        compiler_params=_cparams(("parallel", "parallel")),
        name="dilated_attention",
    )(q, k, v)


def _poolmem_kernel(u_ref, qm_ref, mem_ref, wkv_ref, wpool_ref, pscale_ref, y_ref, kv_s, qa_s, qb_s):
    seq = u_ref.shape[0]
    u = u_ref[...]
    rows = lax.broadcasted_iota(jnp.int32, (seq, POOL_WIDTH), 0)
    grp = lax.broadcasted_iota(jnp.int32, (seq, POOL_WIDTH), 1) // POOL_GROUP

    def shifted(a, kk):
        return jnp.where(rows >= kk, pltpu.roll(a, kk, 0), 0.0)

    s2 = u + shifted(u, 1)
    s4 = s2 + shifted(s2, 2)
    s8 = s4 + shifted(s4, 4)
    s16 = s8 + shifted(s8, 8)
    wsum = jnp.where(grp == 0, s2, jnp.where(grp == 1, s4, jnp.where(grp == 2, s8, s16)))
    win = jnp.where(grp == 0, 2, jnp.where(grp == 1, 4, jnp.where(grp == 2, 8, 16)))
    cnt = jnp.minimum(rows + 1, win).astype(F32)
    pooled = wsum / cnt - u
    mixed = jnp.dot(pooled.astype(BF16), wpool_ref[...], preferred_element_type=F32) * pscale_ref[...]
    y_ref[:, :POOL_WIDTH] = mixed

    kv_s[...] = jnp.dot(mem_ref[...].astype(BF16), wkv_ref[...], preferred_element_type=F32).astype(BF16)
    first = lax.broadcasted_iota(jnp.int32, (seq, LANES), 1) < HEAD_DIM
    for j in range(MEM_WIDTH // LANES):
        qv = qm_ref[:, j * LANES:(j + 1) * LANES]
        qa_s[...] = jnp.where(first, qv, 0.0).astype(BF16)
        qb_s[...] = jnp.where(first, 0.0, qv).astype(BF16)
        kk = kv_s[:, j * LANES:(j + 1) * LANES]
        vv = kv_s[:, MEM_WIDTH + j * LANES:MEM_WIDTH + (j + 1) * LANES]

        def body(it, carry):
            for u in range(MEM_UNROLL):
                qrow = pl.multiple_of((it * MEM_UNROLL + u) * QBLK, QBLK)
                q2 = jnp.concatenate([qa_s[pl.ds(qrow, QBLK), :], qb_s[pl.ds(qrow, QBLK), :]], axis=0)
                o, _ = _two_head_attention(q2, kk, vv, None)
                y_ref[pl.ds(qrow, QBLK), pl.ds(POOL_WIDTH + j * LANES, LANES)] = o
            return carry
        lax.fori_loop(0, seq // (QBLK * MEM_UNROLL), body, 0)


def _pool_and_memory(u, qm, mem2d, wkv_bf, wpool_bd, pscale, batch, seq):
    n = u.shape[0]
    mem_len = mem2d.shape[0] // batch
    row = lambda b: (b, 0)
    fix = lambda b: (0, 0)
    return pl.pallas_call(
        _poolmem_kernel,
        grid=(batch,),
        in_specs=[
            pl.BlockSpec((seq, POOL_WIDTH), row),
            pl.BlockSpec((seq, MEM_WIDTH), row),
            pl.BlockSpec((mem_len, D_MODEL), row),
            pl.BlockSpec((D_MODEL, 2 * MEM_WIDTH), fix),
            pl.BlockSpec((POOL_WIDTH, POOL_WIDTH), fix),
            pl.BlockSpec((1, POOL_WIDTH), fix),
        ],
        out_specs=pl.BlockSpec((seq, POOL_WIDTH + MEM_WIDTH), row),
        out_shape=jax.ShapeDtypeStruct((n, POOL_WIDTH + MEM_WIDTH), F32),
        scratch_shapes=[
            pltpu.VMEM((mem_len, 2 * MEM_WIDTH), BF16),
            pltpu.VMEM((seq, LANES), BF16),
            pltpu.VMEM((seq, LANES), BF16),
        ],
        compiler_params=_cparams(("parallel",)),
        name="pool_memory",
    )(u, qm, mem2d, wkv_bf, wpool_bd, pscale)


def _layer_norm(z, g, b):
    mu = jnp.mean(z, axis=-1, keepdims=True)
    zc = z - mu
    var = jnp.mean(zc * zc, axis=-1, keepdims=True)
    return zc * lax.rsqrt(var + LN_EPS) * g + b


def _store_token_rows(dst_ref, val, first=0):
    tm = val.shape[0]
    for s in range(ROW_VREGS):
        dst_ref[pl.ds(first * ROW_VREGS + s, tm, stride=ROW_VREGS), :] = val[:, s * LANES:(s + 1) * LANES]


def _load_token_rows(src_ref, tm, first=0):
    return jnp.concatenate(
        [src_ref[pl.ds(first * ROW_VREGS + s, tm, stride=ROW_VREGS), :] for s in range(ROW_VREGS)], axis=1)


def _route(x1, wrh, wrl, rbias, ridx_ref, rw_ref):
    tm = x1.shape[0]
    xh = x1.astype(BF16)
    xl = (x1 - xh.astype(F32)).astype(BF16)
    nt = (((1,), (1,)), ((), ()))
    logits = (lax.dot_general(wrh, xh, nt, preferred_element_type=F32)
              + lax.dot_general(wrh, xl, nt, preferred_element_type=F32)
              + lax.dot_general(wrl, xh, nt, preferred_element_type=F32))
    scores = jax.nn.sigmoid(logits)
    biased = scores + rbias

    per = N_EXPERTS // N_GROUPS
    io8 = lax.broadcasted_iota(jnp.int32, (per, tm), 0)
    gscores = []
    for g in range(N_GROUPS):
        vg = biased[g * per:(g + 1) * per]
        m1 = jnp.max(vg, axis=0, keepdims=True)
        i1 = jnp.min(jnp.where(vg == m1, io8, per), axis=0, keepdims=True)
        m2 = jnp.max(jnp.where(io8 == i1, -jnp.inf, vg), axis=0, keepdims=True)
        gscores.append(m1 + m2)
    gs = jnp.concatenate(gscores, axis=0)
    iog = lax.broadcasted_iota(jnp.int32, (N_GROUPS, tm), 0)
    grank = jnp.zeros((N_GROUPS, tm), F32)
    for g in range(N_GROUPS):
        sg = gs[g:g + 1]
        tie = jnp.where(iog > g, 1.0, 0.0)
        grank = grank + jnp.where(sg > gs, 1.0, jnp.where(sg == gs, tie, 0.0))
    gsel = grank < TOPK_GROUPS
    masked = jnp.concatenate(
        [jnp.where(gsel[g:g + 1], biased[g * per:(g + 1) * per], -jnp.inf) for g in range(N_GROUPS)], axis=0)

    ioe = lax.broadcasted_iota(jnp.int32, (N_EXPERTS, tm), 0)
    rank = jnp.zeros((N_EXPERTS, tm), F32)
    for e in range(N_EXPERTS):
        me = masked[e:e + 1]
        tie = jnp.where(ioe > e, 1.0, 0.0)
        rank = rank + jnp.where(me > masked, 1.0, jnp.where(me == masked, tie, 0.0))

    idx_rows, w_rows = [], []
    for kk in range(TOP_K):
        hit = rank == float(kk)
        idx_rows.append(jnp.sum(jnp.where(hit, ioe, 0), axis=0, keepdims=True))
        w_rows.append(jnp.sum(jnp.where(hit, scores, 0.0), axis=0, keepdims=True))
    top_w = jnp.concatenate(w_rows, axis=0)
    denom = jnp.sum(top_w, axis=0, keepdims=True)
    ridx_ref[...] = jnp.concatenate(idx_rows, axis=0)
    rw_ref[...] = top_w / denom * ROUTED_SCALE


def _outproj_kernel(alpha, x_ref, ypm_ref, ydsa_ref, wo_ref, g_ref, b_ref, wrh_ref, wrl_ref, rb_ref,
                    x1_ref, x1g_ref, ridx_ref, rw_ref):
    y = jnp.concatenate([ypm_ref[:, :POOL_WIDTH], ydsa_ref[...], ypm_ref[:, POOL_WIDTH:]], axis=1)
    mix = jnp.dot(y.astype(BF16), wo_ref[...], preferred_element_type=F32)
    x1 = _layer_norm(alpha * x_ref[...] + mix, g_ref[...], b_ref[...])
    x1_ref[...] = x1
    _store_token_rows(x1g_ref, x1)
    _route(x1, wrh_ref[...], wrl_ref[...], rb_ref[...], ridx_ref, rw_ref)


def _out_projection(x2d, ypm, ydsa, wo_bf, g, b, wrh, wrl, rbias, alpha):
    n = x2d.shape[0]
    tm = TM_PROJ
    row = lambda i: (i, 0)
    fix = lambda i: (0, 0)
    col = lambda i: (0, i)
    return pl.pallas_call(
        functools.partial(_outproj_kernel, alpha),
        grid=(n // tm,),
        in_specs=[
            pl.BlockSpec((tm, D_MODEL), row),
            pl.BlockSpec((tm, POOL_WIDTH + MEM_WIDTH), row),
            pl.BlockSpec((tm, DSA_WIDTH), row),
            pl.BlockSpec((D_MODEL, D_MODEL), fix),
            pl.BlockSpec((1, D_MODEL), fix),
            pl.BlockSpec((1, D_MODEL), fix),
            pl.BlockSpec((N_EXPERTS, D_MODEL), fix),
            pl.BlockSpec((N_EXPERTS, D_MODEL), fix),
            pl.BlockSpec((N_EXPERTS, 1), fix),
        ],
        out_specs=[
            pl.BlockSpec((tm, D_MODEL), row),
            pl.BlockSpec((tm * ROW_VREGS, LANES), row),
            pl.BlockSpec((TOP_K, tm), col),
            pl.BlockSpec((TOP_K, tm), col),
        ],
        out_shape=[
            jax.ShapeDtypeStruct((n, D_MODEL), F32),
            jax.ShapeDtypeStruct((n * ROW_VREGS, LANES), F32),
            jax.ShapeDtypeStruct((TOP_K, n), jnp.int32),
            jax.ShapeDtypeStruct((TOP_K, n), F32),
        ],
        compiler_params=_cparams(("parallel",)),
        name="out_projection_router",
    )(x2d, ypm, ydsa, wo_bf, g, b, wrh, wrl, rbias)


def _moe_blocks(chunk):
    return chunk * TOP_K // MOE_BM + N_EXPERTS


def _as_column(w_row):
    bm = w_row.shape[1]
    eye = lax.broadcasted_iota(jnp.int32, (bm, bm), 0) == lax.broadcasted_iota(jnp.int32, (bm, bm), 1)
    return jnp.sum(jnp.where(eye, w_row, 0.0), axis=1, keepdims=True)


def _moe_kernel(eb_ref, nact_ref, tok_ref, wrow_ref, xg_ref, wg_ref, wu_ref, wd_ref, acc_ref,
                rows0, rows1, y0, y1):
    c = pl.program_id(0)
    j = pl.program_id(1)

    @pl.when(j == 0)
    def _():
        acc_ref[...] = jnp.zeros(acc_ref.shape, F32)

    @pl.when(jnp.logical_and(c == 0, j == 0))
    def _():
        for buf in (rows0, rows1, y0, y1):
            buf[...] = jnp.zeros(buf.shape, F32)

    def step(rows_w, rows_r, y_w, y_r):
        scatter0 = j * MOE_BM
        gather0 = scatter0 + MOE_LAG_SCATTER * MOE_BM
        for i in range(MOE_BM):
            src = pl.multiple_of(tok_ref[gather0 + i], ROW_VREGS)
            rows_w[i * ROW_VREGS:(i + 1) * ROW_VREGS, :] = xg_ref[pl.ds(src, ROW_VREGS), :]

        for g0 in range(0, MOE_BM, MOE_UNROLL):
            upd = []
            for i in range(g0, g0 + MOE_UNROLL):
                dst = pl.multiple_of(tok_ref[scatter0 + i], ROW_VREGS)
                upd.append((dst, acc_ref[pl.ds(dst, ROW_VREGS), :] + y_r[i * ROW_VREGS:(i + 1) * ROW_VREGS, :]))
            for dst, val in reversed(upd):
                acc_ref[pl.ds(dst, ROW_VREGS), :] = val

        wcol = _as_column(wrow_ref[pl.ds(j, 1), :])
        sub = MOE_BM // MOE_SPLIT
        for h in range(MOE_SPLIT):
            xb = _load_token_rows(rows_r, sub, h * sub).astype(BF16)
            gg = jnp.dot(xb, wg_ref[...], preferred_element_type=F32)
            uu = jnp.dot(xb, wu_ref[...], preferred_element_type=F32)
            hh = gg * jax.nn.sigmoid(gg) * uu * wcol[h * sub:(h + 1) * sub]
            yy = jnp.dot(hh.astype(BF16), wd_ref[...], preferred_element_type=F32)
            _store_token_rows(y_w, yy, h * sub)

    active = j < nact_ref[c] + MOE_EXTRA_STEPS

    @pl.when(jnp.logical_and(active, j % 2 == 0))
    def _():
        step(rows0, rows1, y1, y0)

    @pl.when(jnp.logical_and(active, j % 2 == 1))
    def _():
        step(rows1, rows0, y0, y1)


def _routed_experts(x1g, chunk_tok, chunk_w, block_expert, n_active, w_gate, w_up, w_down, layer, chunk):
    n_tok = x1g.shape[0] // ROW_VREGS
    n_chunks = n_tok // chunk
    nb = _moe_blocks(chunk)
    steps = nb + MOE_EXTRA_STEPS
    acc_rows = chunk * ROW_VREGS

    def expert_w(c, j, eb, na):
        return (layer, eb[c * nb + jnp.clip(j - MOE_LAG_EXPERT, 0, nb - 1)], 0, 0)

    per_chunk = lambda c, j, eb, na: (c, 0, 0)
    rows = lambda: pltpu.VMEM((MOE_BM * ROW_VREGS, LANES), F32)
    grid_spec = pltpu.PrefetchScalarGridSpec(
        num_scalar_prefetch=2,
        grid=(n_chunks, steps),
        in_specs=[
            pl.BlockSpec((chunk_tok.shape[0] // n_chunks,), lambda c, j, eb, na: (c,), memory_space=pltpu.SMEM),
            pl.BlockSpec((None,) + chunk_w.shape[1:], per_chunk),
            pl.BlockSpec((chunk * ROW_VREGS, LANES), lambda c, j, eb, na: (c, 0), pipeline_mode=pl.Buffered(1)),
            pl.BlockSpec((None, None, D_MODEL, D_EXPERT), expert_w),
            pl.BlockSpec((None, None, D_MODEL, D_EXPERT), expert_w),
            pl.BlockSpec((None, None, D_EXPERT, D_MODEL), expert_w),
        ],
        out_specs=pl.BlockSpec((None, acc_rows, LANES), per_chunk, pipeline_mode=pl.Buffered(1)),
        scratch_shapes=[rows(), rows(), rows(), rows()],
    )
    return pl.pallas_call(
        _moe_kernel,
        grid_spec=grid_spec,
        out_shape=jax.ShapeDtypeStruct((n_chunks, acc_rows, LANES), F32),
        compiler_params=_cparams(("arbitrary", "arbitrary")),
        name="routed_experts",
    )(block_expert, n_active, chunk_tok, chunk_w, x1g, w_gate, w_up, w_down)


def _shared_kernel(alpha, x_ref, acc_ref, wg_ref, wu_ref, wd_ref, g_ref, b_ref, o_ref):
    x1 = x_ref[...]
    tm = x1.shape[0]
    xb = x1.astype(BF16)
    gg = jnp.dot(xb, wg_ref[...], preferred_element_type=F32)
    uu = jnp.dot(xb, wu_ref[...], preferred_element_type=F32)
    hh = (gg * jax.nn.sigmoid(gg) * uu).astype(BF16)
    shared = jnp.dot(hh, wd_ref[...], preferred_element_type=F32)
    routed = _load_token_rows(acc_ref, tm)
    o_ref[...] = _layer_norm(alpha * x1 + (routed + shared), g_ref[...], b_ref[...])


def _shared_and_norm(x1, acc, wsg, wsu, wsd, g, b, alpha, chunk):
    n = x1.shape[0]
    tm = TM_WIDE
    per_chunk = chunk // tm
    row = lambda i: (i, 0)
    fix = lambda i: (0, 0)
    return pl.pallas_call(
        functools.partial(_shared_kernel, alpha),
        grid=(n // tm,),
        in_specs=[
            pl.BlockSpec((tm, D_MODEL), row),
            pl.BlockSpec((None, tm * ROW_VREGS, LANES), lambda i: (i // per_chunk, i % per_chunk, 0)),
            pl.BlockSpec((D_MODEL, D_EXPERT), fix),
            pl.BlockSpec((D_MODEL, D_EXPERT), fix),
            pl.BlockSpec((D_EXPERT, D_MODEL), fix),
            pl.BlockSpec((1, D_MODEL), fix),
            pl.BlockSpec((1, D_MODEL), fix),
        ],
        out_specs=pl.BlockSpec((tm, D_MODEL), row),
        out_shape=jax.ShapeDtypeStruct((n, D_MODEL), F32),
        compiler_params=_cparams(("parallel",)),
        name="shared_expert_norm",
    )(x1, acc, wsg, wsu, wsd, g, b)


def _rope_tables(seq):
    half = ROT_DIM // 2
    pos = jnp.arange(seq, dtype=F32)
    inv = ROPE_THETA ** (-jnp.arange(0, ROT_DIM, 2, dtype=F32) / ROT_DIM)
    ang = pos[:, None] * inv[None, :]
    cos, sin = jnp.cos(ang), jnp.sin(ang)
    pad = HEAD_DIM - ROT_DIM
    ones = jnp.ones((seq, pad), F32)
    zeros = jnp.zeros((seq, pad), F32)
    zh = jnp.zeros((seq, half), F32)
    c_head = jnp.concatenate([cos, cos, ones], axis=1)
    s1_head = jnp.concatenate([-sin, zh, zeros], axis=1)
    s2_head = jnp.concatenate([zh, sin, zeros], axis=1)
    rep = LANES // HEAD_DIM
    return (jnp.tile(c_head, (1, rep)), jnp.tile(s1_head, (1, rep)), jnp.tile(s2_head, (1, rep)))


def _dispatch_lists(ridx, rw, chunk):
    n = ridx.shape[1]
    n_chunks = n // chunk
    e_flat = ridx.T.reshape(n_chunks, chunk * TOP_K)
    w_flat = rw.T.reshape(n_chunks, chunk * TOP_K)
    n_asg = chunk * TOP_K
    nb = _moe_blocks(chunk)
    pos = lax.broadcasted_iota(jnp.int32, e_flat.shape, 1)
    key_sorted, w_sorted = lax.sort((e_flat * n_asg + pos, w_flat), dimension=1, is_stable=False, num_keys=1)
    tok_sorted = (key_sorted % n_asg) // TOP_K * ROW_VREGS
    experts = jnp.arange(N_EXPERTS, dtype=jnp.int32)
    counts = jnp.sum((e_flat[:, :, None] == experts).astype(jnp.int32), axis=1)
    start = jnp.cumsum(counts, axis=1) - counts
    n_blk = (counts + MOE_BM - 1) // MOE_BM
    b_end = jnp.cumsum(n_blk, axis=1)
    b_start = b_end - n_blk
    n_active = b_end[:, -1]

    jb = jnp.arange(nb, dtype=jnp.int32)
    eb = jnp.minimum(jnp.sum((jb[None, :, None] >= b_end[:, None, :]).astype(jnp.int32), axis=2), N_EXPERTS - 1)
    take = lambda a: jnp.take_along_axis(a, eb, axis=1)
    row = (jb[None, :] - take(b_start))[:, :, None] * MOE_BM + jnp.arange(MOE_BM, dtype=jnp.int32)
    valid = (jb[None, :, None] < n_active[:, None, None]) & (row < take(counts)[:, :, None])
    src = jnp.clip(take(start)[:, :, None] + row, 0, n_asg - 1).reshape(n_chunks, -1)
    tok = jnp.where(valid, jnp.take_along_axis(tok_sorted, src, axis=1).reshape(valid.shape), 0)
    wgt = jnp.where(valid, jnp.take_along_axis(w_sorted, src, axis=1).reshape(valid.shape), 0.0)

    def padded(a, before, after):
        pad = lambda k: jnp.zeros((n_chunks, k, MOE_BM), a.dtype)
        return jnp.concatenate([pad(before), a, pad(after)], axis=1)
    tok_blocks = MOE_LAG_SCATTER + nb + MOE_EXTRA_STEPS
    align = SMEM_BLOCK_1D // MOE_BM
    tail = MOE_EXTRA_STEPS + (-tok_blocks) % align
    chunk_tok = padded(tok, MOE_LAG_SCATTER, tail).reshape(-1)
    chunk_w = padded(wgt, MOE_LAG_EXPERT, MOE_EXTRA_STEPS - MOE_LAG_EXPERT)
    return chunk_tok, chunk_w, eb.reshape(-1), n_active


def kernel(x, mem, w_in, w_pool, pool_scale, w_mem_kv, w_out, ln1_g, ln1_b, w_router, router_bias,
           w_gate, w_up, w_down, ws_gate, ws_up, ws_down, ln2_g, ln2_b):
    batch, seq, d = x.shape
    depth = w_in.shape[0]
    assert d == D_MODEL and seq % (16 * QBLK) == 0 and seq % TM_PROJ == 0
    assert (seq // QBLK - 1) % DSA_UNROLL1 == 0 and (seq // QBLK) % max(DSA_UNROLL16, MEM_UNROLL) == 0
    n = batch * seq
    chunk = min(MOE_CHUNK, n)
    assert n % chunk == 0 and chunk % TM_PROJ == 0 and chunk % TM_WIDE == 0 and seq % TM_WIDE == 0
    alpha = float((2 * depth) ** 0.25)

    rope_c, rope_s1, rope_s2 = _rope_tables(seq)
    wg_bf, wu_bf, wd_bf = w_gate.astype(BF16), w_up.astype(BF16), w_down.astype(BF16)
    x2d = x.reshape(n, d)
    mem2d = mem.reshape(batch * mem.shape[1], d)

    for l in range(depth):
        w_in_bf = w_in[l].astype(BF16)
        wkv_bf = w_mem_kv[l].astype(BF16)
        wo_bf = w_out[l].astype(BF16)
        wpool_bd = jax.scipy.linalg.block_diag(*[w_pool[l, g] for g in range(len(POOL_WINDOWS))]).astype(BF16)
        pscale = pool_scale[l].reshape(1, POOL_WIDTH)
        wr_t = w_router[l].T
        wrh = wr_t.astype(BF16)
        wrl = (wr_t - wrh.astype(F32)).astype(BF16)
        rbias = router_bias[l].reshape(N_EXPERTS, 1)

        u, q, k, v, qm = _in_projection(x2d, w_in_bf, rope_c, rope_s1, rope_s2, seq)
        ydsa = _dilated_attention(q, k, v, batch, seq)
        ypm = _pool_and_memory(u, qm, mem2d, wkv_bf, wpool_bd, pscale, batch, seq)
        x1, x1g, ridx, rw = _out_projection(
            x2d, ypm, ydsa, wo_bf, ln1_g[l].reshape(1, d), ln1_b[l].reshape(1, d), wrh, wrl, rbias, alpha)
        chunk_tok, chunk_w, block_expert, n_active = _dispatch_lists(ridx, rw, chunk)
        acc = _routed_experts(x1g, chunk_tok, chunk_w, block_expert, n_active, wg_bf, wu_bf, wd_bf, l, chunk)
        x2d = _shared_and_norm(
            x1, acc, ws_gate[l].astype(BF16), ws_up[l].astype(BF16), ws_down[l].astype(BF16),
            ln2_g[l].reshape(1, d), ln2_b[l].reshape(1, d), alpha, chunk)
    return x2d.reshape(batch, seq, d)
```
